```python
import jax, jax.numpy as jnp
from jax import lax
import numpy as np

D_MODEL = 1024
BATCH = 1
SEQ = 16384
DEPTH = 4
DEC_BATCH = 8
DEC_SEQ = 64
PAST_LEN = 2048

CHUNK = 64
N_MIXERS = 3
POOL_WINDOWS = (2, 4, 8, 16)
POOL_GROUPS = 4
POOL_GC = D_MODEL // POOL_GROUPS
POOL_HIST = max(POOL_WINDOWS) - 1
ATTN_HEADS = 16
ATTN_HD = D_MODEL // ATTN_HEADS
N_PAST_CHUNKS = 8
ATTN_WINDOW = N_PAST_CHUNKS * CHUNK
REL_MIN = -(CHUNK - 1)
REL_MAX = 256
REL_SIZE = REL_MAX - REL_MIN + 1
HG_EXPAND = 128
HG_HEADS = D_MODEL // HG_EXPAND
HG_DK = HG_EXPAND
HG_DV = D_MODEL // HG_HEADS
D_FF = 4 * D_MODEL
EPS = 1e-6
N_POOL = (DEPTH + 2) // 3
N_ATTN = (DEPTH + 1) // 3
N_HGRN = DEPTH // 3

kernel_name = "hybrid_streaming_pool_attn_hgrn2"


def rmsnorm(x, g):
    xf = x.astype(jnp.float32)
    y = xf * lax.rsqrt(jnp.mean(xf * xf, axis=-1, keepdims=True) + EPS)
    return (y * g.astype(jnp.float32)).astype(x.dtype)


def mlp(u, w1, w2):
    h = jnp.square(jax.nn.relu(u @ w1))
    return h @ w2


def pool_mixer(u, hist, p0, w_grp, scale):
    B, L, D = u.shape
    P = hist.shape[1]
    ext = jnp.concatenate([hist.astype(u.dtype), u], axis=1)
    csum = jnp.cumsum(ext.astype(jnp.float32), axis=1)
    csum = jnp.concatenate([jnp.zeros((B, 1, D), jnp.float32), csum], axis=1)
    pos = p0 + jnp.arange(L)
    pooled = []
    for g, w in enumerate(POOL_WINDOWS):
        sl = slice(g * POOL_GC, (g + 1) * POOL_GC)
        win = csum[:, P + 1:, sl] - csum[:, P + 1 - w:P + 1 - w + L, sl]
        cnt = jnp.minimum(pos + 1, w).astype(jnp.float32)[None, :, None]
        pooled.append(win / cnt)
    pooled = jnp.stack(pooled, axis=2)
    diff = pooled - u.astype(jnp.float32).reshape(B, L, POOL_GROUPS, POOL_GC)
    y = jnp.einsum('blgc,gcd->blgd', diff, w_grp.astype(jnp.float32)).reshape(B, L, D)
    y = y * scale.astype(jnp.float32)
    return y.astype(u.dtype), ext[:, -P:]


def band_attend(q, k, v, q_pos, k_pos, rel_bias):
    s = jnp.einsum('bqhd,bkhd->bhqk', q, k).astype(jnp.float32) * (ATTN_HD ** -0.5)
    rel = jnp.clip(q_pos[:, None] - k_pos[None, :], REL_MIN, REL_MAX) - REL_MIN
    s = s + rel_bias[:, rel].astype(jnp.float32)
    qc = q_pos // CHUNK
    kc = k_pos // CHUNK
    ok = (k_pos[None, :] >= 0) & (kc[None, :] <= qc[:, None]) & (kc[None, :] >= qc[:, None] - N_PAST_CHUNKS)
    s = jnp.where(ok, s, -jnp.inf)
    p = jax.nn.softmax(s, axis=-1)
    return jnp.einsum('bhqk,bkhd->bqhd', p.astype(v.dtype), v)


def attn_qkv(u, wqkv):
    B, L, _ = u.shape
    qkv = (u @ wqkv).reshape(B, L, 3, ATTN_HEADS, ATTN_HD)
    return qkv[:, :, 0], qkv[:, :, 1], qkv[:, :, 2]


def attn_prompt(u, wqkv, wo, rel_bias):
    B, L, D = u.shape
    q, k, v = attn_qkv(u, wqkv)
    n_chunks = L // CHUNK
    pad = ((0, 0), (ATTN_WINDOW, 0), (0, 0), (0, 0))
    kp = jnp.pad(k, pad)
    vp = jnp.pad(v, pad)
    band = ATTN_WINDOW + CHUNK

    def one_chunk(c):
        start = c * CHUNK
        qc = lax.dynamic_slice_in_dim(q, start, CHUNK, axis=1)
        kc = lax.dynamic_slice_in_dim(kp, start, band, axis=1)
        vc = lax.dynamic_slice_in_dim(vp, start, band, axis=1)
        q_pos = start + jnp.arange(CHUNK)
        k_pos = start - ATTN_WINDOW + jnp.arange(band)
        return band_attend(qc, kc, vc, q_pos, k_pos, rel_bias)

    o = lax.map(one_chunk, jnp.arange(n_chunks))
    o = jnp.moveaxis(o, 0, 1).reshape(B, L, D)
    keep = min(ATTN_WINDOW, L)
    return o @ wo, k[:, -keep:], v[:, -keep:]


def attn_sample(u, cache_k, cache_v, wqkv, wo, rel_bias):
    B, L, D = u.shape
    q, k, v = attn_qkv(u, wqkv)
    P = cache_k.shape[1]
    kb = jnp.concatenate([cache_k.astype(k.dtype), k], axis=1)
    vb = jnp.concatenate([cache_v.astype(v.dtype), v], axis=1)
    q_pos = PAST_LEN + jnp.arange(L)
    k_pos = PAST_LEN - P + jnp.arange(P + L)
    o = band_attend(q, kb, vb, q_pos, k_pos, rel_bias).reshape(B, L, D)
    return o @ wo, k, v


def hgrn_chunk_step(S, inp):
    q, k, v, logf = inp
    C = q.shape[1]
    b = jnp.cumsum(logf, axis=1)
    causal = jnp.tril(jnp.ones((C, C), dtype=bool))[None, :, :, None, None]
    decay = jnp.exp(jnp.where(causal, b[:, :, None] - b[:, None, :], -jnp.inf))
    A = jnp.einsum('bthk,bshk,btshk->bhts', q, k, decay)
    o = jnp.einsum('bhts,bshv->bthv', A, v) + jnp.einsum('bthk,bhkv->bthv', q * jnp.exp(b), S)
    b_last = b[:, -1]
    k_dec = k * jnp.exp(b_last[:, None] - b)
    S_new = jnp.exp(b_last)[..., None] * S + jnp.einsum('bshk,bshv->bhkv', k_dec, v)
    return S_new, o


def hgrn_mixer(u, S0, w_in, w_o, g_norm, lb, chunk):
    B, L, D = u.shape
    proj = (u @ w_in).astype(jnp.float32)
    zq, zf, zi, zg = jnp.split(proj, 4, axis=-1)
    q = jax.nn.silu(zq)
    logf = jnp.logaddexp(jnp.log(lb), jnp.log1p(-lb) + jax.nn.log_sigmoid(zf))
    k = -jnp.expm1(logf)

    def to_chunks(t, dh):
        return jnp.moveaxis(t.reshape(B, L // chunk, chunk, HG_HEADS, dh), 1, 0)

    xs = (to_chunks(q, HG_DK), to_chunks(k, HG_DK), to_chunks(zi, HG_DV), to_chunks(logf, HG_DK))
    S, o = lax.scan(hgrn_chunk_step, S0.astype(jnp.float32), xs)
    o = jnp.moveaxis(o, 0, 1).reshape(B, L, HG_HEADS, HG_DV)
    o = rmsnorm(o, g_norm) * jax.nn.silu(zg.reshape(B, L, HG_HEADS, HG_DV))
    y = o.reshape(B, L, D).astype(u.dtype) @ w_o
    return y, S.astype(S0.dtype)


def setup_inputs(seed: int = 0) -> dict:
    key = jax.random.key(seed)
    ks = jax.random.split(key, 20)

    def nrm(k, shape, s):
        return jax.random.normal(k, shape, jnp.float32) * s

    attn_cache = min(ATTN_WINDOW, PAST_LEN)
    D = D_MODEL
    return {
        "x_prompt": nrm(ks[0], (BATCH, SEQ, D), 1.0),
        "x_sample": nrm(ks[1], (DEC_BATCH, DEC_SEQ, D), 1.0),
        "cache_pool": nrm(ks[2], (N_POOL, DEC_BATCH, POOL_HIST, D), 1.0),
        "cache_attn_k": nrm(ks[3], (N_ATTN, DEC_BATCH, attn_cache, ATTN_HEADS, ATTN_HD), 1.0),
        "cache_attn_v": nrm(ks[4], (N_ATTN, DEC_BATCH, attn_cache, ATTN_HEADS, ATTN_HD), 1.0),
        "state_hgrn": nrm(ks[5], (N_HGRN, DEC_BATCH, HG_HEADS, HG_DK, HG_DV), 0.5),
        "norm_mix": 1.0 + nrm(ks[6], (DEPTH, D), 0.05),
        "norm_ffn": 1.0 + nrm(ks[7], (DEPTH, D), 0.05),
        "norm_final": 1.0 + nrm(ks[8], (D,), 0.05),
        "pool_w": nrm(ks[9], (N_POOL, POOL_GROUPS, POOL_GC, POOL_GC), POOL_GC ** -0.5),
        "pool_scale": 1.0 + nrm(ks[10], (N_POOL, D), 0.1),
        "attn_wqkv": nrm(ks[11], (N_ATTN, D, 3 * D), D ** -0.5),
        "attn_wo": nrm(ks[12], (N_ATTN, D, D), D ** -0.5),
        "attn_rel_bias": nrm(ks[13], (N_ATTN, ATTN_HEADS, REL_SIZE), 0.5),
        "hgrn_w_in": nrm(ks[14], (N_HGRN, D, 4 * D), D ** -0.5),
        "hgrn_w_o": nrm(ks[15], (N_HGRN, D, D), D ** -0.5),
        "hgrn_gnorm": 1.0 + nrm(ks[16], (N_HGRN, HG_DV), 0.05),
        "hgrn_lower_bounds": nrm(ks[17], (DEPTH, HG_HEADS * HG_DK), 0.5),
        "mlp_w1": nrm(ks[18], (DEPTH, D, D_FF), D ** -0.5),
        "mlp_w2": nrm(ks[19], (DEPTH, D_FF, D), 0.5 * D_FF ** -0.5),
    }


def reference(x_prompt, x_sample, cache_pool, cache_attn_k, cache_attn_v, state_hgrn,
              norm_mix, norm_ffn, norm_final, pool_w, pool_scale, attn_wqkv, attn_wo, attn_rel_bias,
              hgrn_w_in, hgrn_w_o, hgrn_gnorm, hgrn_lower_bounds, mlp_w1, mlp_w2):
    lb_soft = jax.nn.softmax(hgrn_lower_bounds.astype(jnp.float32), axis=0)
    lb_all = jnp.cumsum(lb_soft, axis=0) - lb_soft[0]

    xp, xs = x_prompt, x_sample
    B = xp.shape[0]
    pool_p, pool_s, k_p, k_s, v_p, v_s, h_p, h_s = [], [], [], [], [], [], [], []
    for i in range(DEPTH):
        kind, j = i % N_MIXERS, i // N_MIXERS
        up = rmsnorm(xp, norm_mix[i])
        us = rmsnorm(xs, norm_mix[i])
        if kind == 0:
            zero_hist = jnp.zeros((B, POOL_HIST, D_MODEL), up.dtype)
            mp, hp_ = pool_mixer(up, zero_hist, 0, pool_w[j], pool_scale[j])
            ms, hs_ = pool_mixer(us, cache_pool[j], PAST_LEN, pool_w[j], pool_scale[j])
            pool_p.append(hp_)
            pool_s.append(hs_)
        elif kind == 1:
            mp, kp_, vp_ = attn_prompt(up, attn_wqkv[j], attn_wo[j], attn_rel_bias[j])
            ms, ks_, vs_ = attn_sample(us, cache_attn_k[j], cache_attn_v[j], attn_wqkv[j], attn_wo[j], attn_rel_bias[j])
            k_p.append(kp_)
            v_p.append(vp_)
            k_s.append(ks_)
            v_s.append(vs_)
        else:
            S0 = jnp.zeros((B, HG_HEADS, HG_DK, HG_DV), state_hgrn.dtype)
            mp, Sp = hgrn_mixer(up, S0, hgrn_w_in[j], hgrn_w_o[j], hgrn_gnorm[j], lb_all[i], CHUNK)
            ms, Ss = hgrn_mixer(us, state_hgrn[j], hgrn_w_in[j], hgrn_w_o[j], hgrn_gnorm[j], lb_all[i], xs.shape[1])
            h_p.append(Sp)
            h_s.append(Ss)
        xp = xp + mp
        xs = xs + ms
        xp = xp + mlp(rmsnorm(xp, norm_ffn[i]), mlp_w1[i], mlp_w2[i])
        xs = xs + mlp(rmsnorm(xs, norm_ffn[i]), mlp_w1[i], mlp_w2[i])

    y_prompt = rmsnorm(xp, norm_final)
    y_sample = rmsnorm(xs, norm_final)
    new_pool_prompt = jnp.stack(pool_p)
    new_pool_sample = jnp.stack(pool_s)
    new_k_prompt = jnp.stack(k_p)
    new_k_sample = jnp.stack(k_s)
    new_v_prompt = jnp.stack(v_p)
    new_v_sample = jnp.stack(v_s)
    new_hgrn_prompt = jnp.stack(h_p)
    new_hgrn_sample = jnp.stack(h_s)
    return (y_prompt, y_sample, new_pool_prompt, new_pool_sample, new_k_prompt, new_k_sample,
            new_v_prompt, new_v_sample, new_hgrn_prompt, new_hgrn_sample)
```

```python
import functools
import math

import jax
import jax.numpy as jnp
from jax import lax
from jax.experimental import pallas as pl
from jax.experimental.pallas import tpu as pltpu

F32 = jnp.float32
BF16 = jnp.bfloat16

D_MODEL = 1024
DEPTH = 4
CHUNK = 64
N_MIXERS = 3
PAST_LEN = 2048
EPS = 1e-6
POOL_WINDOWS = (2, 4, 8, 16)
POOL_GROUPS = 4
POOL_GC = D_MODEL // POOL_GROUPS
POOL_HIST = max(POOL_WINDOWS) - 1
POOL_HALO = POOL_HIST + 1
ATTN_HEADS = 16
ATTN_HD = D_MODEL // ATTN_HEADS
N_PAST_CHUNKS = 8
ATTN_WINDOW = N_PAST_CHUNKS * CHUNK
REL_MIN = -(CHUNK - 1)
REL_MAX = 256
HEAD_PAIRS = ATTN_HEADS // 2
PAIR_W = 2 * ATTN_HD
HG_HEADS = 8
HG_DK = D_MODEL // HG_HEADS
HG_DV = D_MODEL // HG_HEADS
HG_DIAG = 8
D_FF = 4 * D_MODEL

V7X_VMEM_BYTES = 64 * 1024 * 1024
VMEM_LIMIT_BYTES = V7X_VMEM_BYTES - 8 * 1024 * 1024

NEG_INF = float("-inf")


def _params():
    return pltpu.CompilerParams(dimension_semantics=("arbitrary", "arbitrary"),
                                vmem_limit_bytes=VMEM_LIMIT_BYTES)


def _whole(shape):
    zeros = (0,) * len(shape)
    return pl.BlockSpec(shape, lambda s, t: zeros)


def _rms(x, g):
    ms = jnp.mean(x * x, axis=-1, keepdims=True)
    return x * lax.rsqrt(ms + EPS) * g


def _dot(a, b):
    return jnp.dot(a, b, preferred_element_type=F32)


def _dot_nt(a, b):
    return lax.dot_general(a, b, (((1,), (1,)), ((), ())), preferred_element_type=F32)


def _dot_tn(a, b):
    return lax.dot_general(a, b, (((0,), (0,)), ((), ())), preferred_element_type=F32)


def _mlp_kernel(x_ref, g_ref, w1_ref, w2_ref, gf_ref, o_ref, *, final_norm):
    x = x_ref[0]
    v = _rms(x, g_ref[...]).astype(BF16)
    h = _dot(v, w1_ref[...])
    h = jnp.square(jnp.maximum(h, 0.0)).astype(BF16)
    y = x + _dot(h, w2_ref[...])
    if final_norm:
        y = _rms(y, gf_ref[...])
    o_ref[0] = y


def _mlp(x, g, w1, w2, gf, *, tile, final_norm):
    S, L, D = x.shape
    assert L % tile == 0
    tok = pl.BlockSpec((1, tile, D), lambda s, t: (s, t, 0))
    return pl.pallas_call(
        functools.partial(_mlp_kernel, final_norm=final_norm),
        grid=(S, L // tile),
        in_specs=[tok, _whole((1, D)), _whole((D, D_FF)), _whole((D_FF, D)), _whole((1, D))],
        out_specs=tok,
        out_shape=jax.ShapeDtypeStruct((S, L, D), F32),
        compiler_params=_params(),
        name="mlp",
    )(x, g, w1, w2, gf)


def _pool_kernel(x_ref, hist_ref, g_ref, w_ref, sc_ref, o_ref, cache_ref, ext_ref, *, tile, p0):
    t = pl.program_id(1)

    @pl.when(t == 0)
    def _():
        ext_ref[0:POOL_HALO, :] = hist_ref[0]

    x = x_ref[0]
    u = _rms(x, g_ref[...])
    ext_ref[POOL_HALO:POOL_HALO + tile, :] = u

    pos = p0 + t * tile + lax.broadcasted_iota(jnp.int32, (tile, 1), 0)
    ys = []
    for gi, w in enumerate(POOL_WINDOWS):
        c0, c1 = gi * POOL_GC, (gi + 1) * POOL_GC
        win = u[:, c0:c1]
        for j in range(1, w):
            win = win + ext_ref[POOL_HALO - j:POOL_HALO - j + tile, c0:c1]
        cnt = jnp.minimum(pos + 1, w).astype(F32)
        diff = win / cnt - u[:, c0:c1]
        ys.append(_dot(diff.astype(BF16), w_ref[gi]))
    y = jnp.concatenate(ys, axis=1) * sc_ref[...]
    o_ref[0] = x + y

    tail = ext_ref[tile:tile + POOL_HALO, :]
    cache_ref[0] = tail
    ext_ref[0:POOL_HALO, :] = tail


def _pool(x, hist, g, w, sc, *, tile, p0):
    S, L, D = x.shape
    assert L % tile == 0 and tile >= POOL_HALO
    tok = pl.BlockSpec((1, tile, D), lambda s, t: (s, t, 0))
    seq = pl.BlockSpec((1, POOL_HALO, D), lambda s, t: (s, 0, 0))
    out, cache = pl.pallas_call(
        functools.partial(_pool_kernel, tile=tile, p0=p0),
        grid=(S, L // tile),
        in_specs=[tok, seq, _whole((1, D)), _whole((POOL_GROUPS, POOL_GC, POOL_GC)), _whole((1, D))],
        out_specs=[tok, seq],
        out_shape=[jax.ShapeDtypeStruct((S, L, D), F32), jax.ShapeDtypeStruct((S, POOL_HALO, D), F32)],
        scratch_shapes=[pltpu.VMEM((POOL_HALO + tile, D), F32)],
        compiler_params=_params(),
        name="pool",
    )(x, hist, g, w, sc)
    return out, cache[:, 1:, :]


def _attn_bias(rel_bias, tile):
    nb = ATTN_WINDOW // tile + 1
    r = jnp.arange(tile)[:, None]
    j = jnp.arange(tile)[None, :]
    blocks = []
    for d in range(nb):
        rel = d * tile + r - j
        idx = jnp.clip(rel, REL_MIN, REL_MAX) - REL_MIN
        dchunk = (d * tile) // CHUNK + r // CHUNK - j // CHUNK
        ok = (dchunk >= 0) & (dchunk <= N_PAST_CHUNKS)
        b = rel_bias[:, idx].astype(F32)
        blocks.append(jnp.where(ok[None], b, NEG_INF))
    b = jnp.stack(blocks, axis=1)
    b = b.reshape(HEAD_PAIRS, 2, nb, tile, tile)
    return jnp.concatenate([b[:, 0], b[:, 1]], axis=-1)


def _attn_kernel(x_ref, hk_ref, hv_ref, g_ref, wqkv_ref, wo_ref, bias_ref,
                 o_ref, ko_ref, vo_ref, kt_ring, v_ring, s_buf, *, tile, mask_start):
    D = D_MODEL
    nb = ATTN_WINDOW // tile + 1
    t = pl.program_id(1)

    row_lo = lax.broadcasted_iota(jnp.int32, (PAIR_W, tile), 0) < ATTN_HD
    lane_lo = lax.broadcasted_iota(jnp.int32, (tile, PAIR_W), 1) < ATTN_HD

    def store_block(slot, k, v):
        kt = k.T.astype(BF16)
        vb = v.astype(BF16)
        zk = jnp.zeros((PAIR_W, tile), BF16)
        zv = jnp.zeros((tile, PAIR_W), BF16)
        for p in range(HEAD_PAIRS):
            ktp = kt[p * PAIR_W:(p + 1) * PAIR_W, :]
            kt_ring[slot, p] = jnp.concatenate(
                [jnp.where(row_lo, ktp, zk), jnp.where(row_lo, zk, ktp)], axis=1)
            vp = vb[:, p * PAIR_W:(p + 1) * PAIR_W]
            v_ring[slot, p] = jnp.concatenate(
                [jnp.where(lane_lo, vp, zv), jnp.where(lane_lo, zv, vp)], axis=0)

    @pl.when(t == 0)
    def _():
        for i in range(nb - 1):
            store_block(i + 1, hk_ref[0, i * tile:(i + 1) * tile, :], hv_ref[0, i * tile:(i + 1) * tile, :])

    x = x_ref[0]
    u = _rms(x, g_ref[...]).astype(BF16)
    qkv = _dot(u, wqkv_ref[...])
    q = (qkv[:, :D] * (ATTN_HD ** -0.5)).astype(BF16)
    k = qkv[:, D:2 * D]
    v = qkv[:, 2 * D:]
    ko_ref[0] = k
    vo_ref[0] = v
    store_block(lax.rem(t, nb), k, v)

    outs = []
    for p in range(HEAD_PAIRS):
        qp = q[:, p * PAIR_W:(p + 1) * PAIR_W]
        m_a = jnp.full((tile, 1), NEG_INF, F32)
        m_b = jnp.full((tile, 1), NEG_INF, F32)
        for d in range(nb):
            slot = lax.rem(t + (nb - d), nb)
            s = _dot(qp, kt_ring[slot, p]) + bias_ref[p, d]
            if mask_start and d > 0:
                s = s + jnp.where(t >= d, 0.0, NEG_INF)
            s_buf[d] = s
            m_a = jnp.maximum(m_a, jnp.max(s[:, :tile], axis=-1, keepdims=True))
            m_b = jnp.maximum(m_b, jnp.max(s[:, tile:], axis=-1, keepdims=True))
        l_a = jnp.zeros((tile, 1), F32)
        l_b = jnp.zeros((tile, 1), F32)
        acc = jnp.zeros((tile, PAIR_W), F32)
        for d in range(nb):
            slot = lax.rem(t + (nb - d), nb)
            s = s_buf[d]
            pa = jnp.exp(s[:, :tile] - m_a)
            pb = jnp.exp(s[:, tile:] - m_b)
            l_a = l_a + jnp.sum(pa, axis=-1, keepdims=True)
            l_b = l_b + jnp.sum(pb, axis=-1, keepdims=True)
            pr = jnp.concatenate([pa, pb], axis=1).astype(BF16)
            acc = acc + _dot(pr, v_ring[slot, p])
        inv = jnp.where(lane_lo, 1.0 / l_a, 1.0 / l_b)
        outs.append((acc * inv).astype(BF16))
    o = jnp.concatenate(outs, axis=1)
    o_ref[0] = x + _dot(o, wo_ref[...])


def _attn(x, hist_k, hist_v, g, wqkv, wo, bias, *, tile, keep, mask_start):
    S, L, D = x.shape
    nb = ATTN_WINDOW // tile + 1
    nt = L // tile
    nkeep = keep // tile
    assert L % tile == 0 and tile % CHUNK == 0 and ATTN_WINDOW % tile == 0 and keep % tile == 0
    tok = pl.BlockSpec((1, tile, D), lambda s, t: (s, t, 0))
    hist = pl.BlockSpec((1, ATTN_WINDOW, D), lambda s, t: (s, 0, 0))
    kv_out = pl.BlockSpec((1, tile, D), lambda s, t: (s, jnp.maximum(t - (nt - nkeep), 0), 0))
    return pl.pallas_call(
        functools.partial(_attn_kernel, tile=tile, mask_start=mask_start),
        grid=(S, nt),
        in_specs=[tok, hist, hist, _whole((1, D)), _whole((D, 3 * D)), _whole((D, D)),
                  _whole((HEAD_PAIRS, nb, tile, 2 * tile))],
        out_specs=[tok, kv_out, kv_out],
        out_shape=[jax.ShapeDtypeStruct((S, L, D), F32),
                   jax.ShapeDtypeStruct((S, keep, D), F32),
                   jax.ShapeDtypeStruct((S, keep, D), F32)],
        scratch_shapes=[pltpu.VMEM((nb, HEAD_PAIRS, PAIR_W, 2 * tile), BF16),
                        pltpu.VMEM((nb, HEAD_PAIRS, 2 * tile, PAIR_W), BF16),
                        pltpu.VMEM((nb, tile, 2 * tile), F32)],
        compiler_params=_params(),
        name="attn",
    )(x, hist_k, hist_v, g, wqkv, wo, bias)


def _split3(x):
    hi = x.astype(BF16)
    r1 = x - hi.astype(F32)
    mid = r1.astype(BF16)
    lo = (r1 - mid.astype(F32)).astype(BF16)
    return hi, mid, lo


def _tile_row(x, n, s):
    C, W = x.shape
    xr = x.reshape(C // n, n, W)
    return jnp.broadcast_to(xr[:, s:s + 1, :], (C // n, n, W)).reshape(C, W)


def _hgrn_kernel(x_ref, s0_ref, g_ref, win_ref, wo_ref, gn_ref, lbp_ref,
                 o_ref, so_ref, st_ref, og_ref, *, chunk, layer):
    D = D_MODEL
    C = chunk
    t = pl.program_id(1)
    nt = pl.num_programs(1)

    @pl.when(t == 0)
    def _():
        for h in range(HG_HEADS):
            st_ref[h] = s0_ref[0, h].T

    x = x_ref[0]
    u = _rms(x, g_ref[...]).astype(BF16)
    proj = _dot(u, win_ref[...])

    lbp = lbp_ref[...]
    e = jnp.exp(lbp - jnp.max(lbp, axis=0, keepdims=True))
    sm = e / jnp.sum(e, axis=0, keepdims=True)
    lb = jnp.zeros((1, D), F32)
    for i in range(1, layer + 1):
        lb = lb + sm[i:i + 1, :]
    log_lb = jnp.log(lb)
    log_1mlb = jnp.log1p(-lb)

    zf = proj[:, D:2 * D]
    a = jnp.exp(-jnp.abs(zf))
    r = 1.0 / (1.0 + a)
    log_sig = jnp.minimum(zf, 0.0) - jnp.log1p(a)
    bb = log_1mlb + log_sig
    logf = jnp.maximum(log_lb, bb) + jnp.log1p(jnp.exp(-jnp.abs(log_lb - bb)))
    kk_all = (1.0 - lb) * jnp.where(zf >= 0.0, a * r, r)

    ri = lax.broadcasted_iota(jnp.int32, (C, C), 0)
    ci = lax.broadcasted_iota(jnp.int32, (C, C), 1)
    tri = (ri >= ci).astype(BF16)
    hi, mid, lo = _split3(logf)
    b_all = _dot(tri, hi) + _dot(tri, mid) + _dot(tri, lo)

    rowid = lax.broadcasted_iota(jnp.int32, (C, 1), 0)
    ones = jnp.ones((HG_DK, HG_DV), BF16)
    levels = []
    hl = C // 2
    while hl >= HG_DIAG:
        upper = (rowid % (2 * hl)) >= hl
        same = (ri // (2 * hl)) == (ci // (2 * hl))
        levels.append((hl, upper, same))
        hl //= 2

    for h in range(HG_HEADS):
        c0, c1 = h * HG_DK, (h + 1) * HG_DK
        zq = proj[:, c0:c1]
        vv = proj[:, 2 * D + c0:2 * D + c1]
        zg = proj[:, 3 * D + c0:3 * D + c1]
        q = zq * (1.0 / (1.0 + jnp.exp(-zq)))
        kk = kk_all[:, c0:c1]
        b = b_all[:, c0:c1]
        st = st_ref[h]

        b_last = b[C - 1:C, :]
        o = _dot_nt((q * jnp.exp(b)).astype(BF16), st.astype(BF16))
        vb = vv.astype(BF16)
        kdec = (kk * jnp.exp(b_last - b)).astype(BF16)
        st_ref[h] = st * jnp.exp(b_last) + _dot_tn(vb, kdec)

        amat = jnp.zeros((C, C), F32)
        for hl, upper, same in levels:
            bref = _tile_row(b, 2 * hl, hl - 1)
            qt = q * jnp.exp(jnp.where(upper, b - bref, NEG_INF))
            kt = kk * jnp.exp(jnp.where(upper, NEG_INF, bref - b))
            amat = amat + jnp.where(same, _dot_nt(qt.astype(BF16), kt.astype(BF16)), 0.0)
        o = o + _dot(amat.astype(BF16), vb)

        sub = rowid % HG_DIAG
        for s in range(HG_DIAG):
            dec = jnp.exp(jnp.where(sub >= s, b - _tile_row(b, HG_DIAG, s), NEG_INF))
            ps = q * _tile_row(kk, HG_DIAG, s) * dec
            o = o + _dot(ps.astype(BF16), ones) * _tile_row(vv, HG_DIAG, s)

        on = o * lax.rsqrt(jnp.mean(o * o, axis=-1, keepdims=True) + EPS) * gn_ref[...]
        og_ref[:, c0:c1] = (on * (zg * (1.0 / (1.0 + jnp.exp(-zg))))).astype(BF16)

    o_ref[0] = x + _dot(og_ref[...], wo_ref[...])

    @pl.when(t == nt - 1)
    def _():
        for h in range(HG_HEADS):
            so_ref[0, h] = st_ref[h].T


def _hgrn(x, s0, g, w_in, w_o, gn, lbp, *, chunk, layer):
    S, L, D = x.shape
    assert L % chunk == 0 and chunk % (2 * HG_DIAG) == 0
    tok = pl.BlockSpec((1, chunk, D), lambda s, t: (s, t, 0))
    state = pl.BlockSpec((1, HG_HEADS, HG_DK, HG_DV), lambda s, t: (s, 0, 0, 0))
    return pl.pallas_call(
        functools.partial(_hgrn_kernel, chunk=chunk, layer=layer),
        grid=(S, L // chunk),
        in_specs=[tok, state, _whole((1, D)), _whole((D, 4 * D)), _whole((D, D)), _whole((1, HG_DV)),
                  _whole((DEPTH, D))],
        out_specs=[tok, state],
        out_shape=[jax.ShapeDtypeStruct((S, L, D), F32),
                   jax.ShapeDtypeStruct((S, HG_HEADS, HG_DK, HG_DV), F32)],
        scratch_shapes=[pltpu.VMEM((HG_HEADS, HG_DV, HG_DK), F32),
                        pltpu.VMEM((chunk, D), BF16)],
        compiler_params=_params(),
        name="hgrn",
    )(x, s0, g, w_in, w_o, gn, lbp)


MLP_TILE = (512, 512)
POOL_TILE = (512, 64)
ATTN_TILE = (128, 64)
HGRN_CHUNK = (128, 64)


def kernel(x_prompt, x_sample, cache_pool, cache_attn_k, cache_attn_v, state_hgrn, norm_mix, norm_ffn,
           norm_final, pool_w, pool_scale, attn_wqkv, attn_wo, attn_rel_bias, hgrn_w_in, hgrn_w_o,
           hgrn_gnorm, hgrn_lower_bounds, mlp_w1, mlp_w2):
    B, L, D = x_prompt.shape
    BS, LS, _ = x_sample.shape
    xp = x_prompt
    xs = x_sample.reshape(1, BS * LS, D)
    gf = norm_final.reshape(1, D)

    pool_p, pool_s, k_p, k_s, v_p, v_s, h_p, h_s = [], [], [], [], [], [], [], []
    for i in range(DEPTH):
        kind, j = i % N_MIXERS, i // N_MIXERS
        g = norm_mix[i].reshape(1, D)
        xs = xs.reshape(BS, LS, D)
        if kind == 0:
            w = pool_w[j].astype(BF16)
            sc = pool_scale[j].reshape(1, D)
            hist_p = jnp.zeros((B, POOL_HALO, D), F32)
            hist_s = jnp.pad(cache_pool[j], ((0, 0), (1, 0), (0, 0)))
            xp, cp = _pool(xp, hist_p, g, w, sc, tile=POOL_TILE[0], p0=0)
            xs, cs = _pool(xs, hist_s, g, w, sc, tile=POOL_TILE[1], p0=PAST_LEN)
            pool_p.append(cp)
            pool_s.append(cs)
        elif kind == 1:
            wqkv = attn_wqkv[j].astype(BF16)
            wo = attn_wo[j].astype(BF16)
            keep_p = min(ATTN_WINDOW, L)
            zeros = jnp.zeros((B, ATTN_WINDOW, D), F32)
            xp, kp, vp = _attn(xp, zeros, zeros, g, wqkv, wo, _attn_bias(attn_rel_bias[j], ATTN_TILE[0]),
                               tile=ATTN_TILE[0], keep=keep_p, mask_start=True)
            ck = cache_attn_k[j].reshape(BS, ATTN_WINDOW, D)
            cv = cache_attn_v[j].reshape(BS, ATTN_WINDOW, D)
            xs, ks, vs = _attn(xs, ck, cv, g, wqkv, wo, _attn_bias(attn_rel_bias[j], ATTN_TILE[1]),
                               tile=ATTN_TILE[1], keep=LS, mask_start=False)
            k_p.append(kp.reshape(B, keep_p, ATTN_HEADS, ATTN_HD))
            v_p.append(vp.reshape(B, keep_p, ATTN_HEADS, ATTN_HD))
            k_s.append(ks.reshape(BS, LS, ATTN_HEADS, ATTN_HD))
            v_s.append(vs.reshape(BS, LS, ATTN_HEADS, ATTN_HD))
        else:
            w_in = hgrn_w_in[j].astype(BF16)
            w_o = hgrn_w_o[j].astype(BF16)
            gn = hgrn_gnorm[j].reshape(1, HG_DV)
            s0 = jnp.zeros((B, HG_HEADS, HG_DK, HG_DV), F32)
            xp, sp = _hgrn(xp, s0, g, w_in, w_o, gn, hgrn_lower_bounds, chunk=HGRN_CHUNK[0], layer=i)
            xs, ss = _hgrn(xs, state_hgrn[j], g, w_in, w_o, gn, hgrn_lower_bounds, chunk=HGRN_CHUNK[1],
                           layer=i)
            h_p.append(sp)
            h_s.append(ss)
        gm = norm_ffn[i].reshape(1, D)
        w1 = mlp_w1[i].astype(BF16)
        w2 = mlp_w2[i].astype(BF16)
        last = i == DEPTH - 1
        xp = _mlp(xp, gm, w1, w2, gf, tile=MLP_TILE[0], final_norm=last)
        xs = _mlp(xs.reshape(1, BS * LS, D), gm, w1, w2, gf, tile=MLP_TILE[1], final_norm=last)

    return (xp, xs.reshape(BS, LS, D), jnp.stack(pool_p), jnp.stack(pool_s), jnp.stack(k_p), jnp.stack(k_s),
            jnp.stack(v_p), jnp.stack(v_s), jnp.stack(h_p), jnp.stack(h_s))
```

```python
import functools
import math

import jax
import jax.numpy as jnp
from jax import lax
from jax.experimental import pallas as pl
from jax.experimental.pallas import tpu as pltpu

F32 = jnp.float32
BF16 = jnp.bfloat16

D_MODEL = 1024
DEPTH = 4
CHUNK = 64
N_MIXERS = 3
PAST_LEN = 2048
EPS = 1e-6
POOL_WINDOWS = (2, 4, 8, 16)
POOL_GROUPS = 4
POOL_GC = D_MODEL // POOL_GROUPS
POOL_HIST = max(POOL_WINDOWS) - 1
POOL_HALO = POOL_HIST + 1
ATTN_HEADS = 16
ATTN_HD = D_MODEL // ATTN_HEADS
N_PAST_CHUNKS = 8
ATTN_WINDOW = N_PAST_CHUNKS * CHUNK
REL_MIN = -(CHUNK - 1)
REL_MAX = 256
REL_SIZE = REL_MAX - REL_MIN + 1
HEAD_PAIRS = ATTN_HEADS // 2
PAIR_W = 2 * ATTN_HD
HG_HEADS = 8
HG_DK = D_MODEL // HG_HEADS
HG_DV = D_MODEL // HG_HEADS
HG_DIAG = 8
HG_SAFE_LOG2 = 100.0
LOG2E = math.log2(math.e)
D_FF = 4 * D_MODEL

V7X_VMEM_BYTES = 64 * 1024 * 1024
VMEM_LIMIT_BYTES = V7X_VMEM_BYTES - 8 * 1024 * 1024

NEG_INF = float("-inf")


def _params():
    return pltpu.CompilerParams(dimension_semantics=("arbitrary", "arbitrary"),
                                vmem_limit_bytes=VMEM_LIMIT_BYTES)


def _whole(shape):
    zeros = (0,) * len(shape)
    return pl.BlockSpec(shape, lambda s, t: zeros)


def _rms(x, g):
    ms = jnp.mean(x * x, axis=-1, keepdims=True)
    return x * lax.rsqrt(ms + EPS) * g


def _dot(a, b):
    return jnp.dot(a, b, preferred_element_type=F32)


def _dot_tn(a, b):
    return lax.dot_general(a, b, (((0,), (0,)), ((), ())), preferred_element_type=F32)


def _split3(x):
    hi = x.astype(BF16)
    r1 = x - hi.astype(F32)
    mid = r1.astype(BF16)
    lo = (r1 - mid.astype(F32)).astype(BF16)
    return hi, mid, lo


def _mlp_kernel(x_ref, g_ref, w1_ref, w2_ref, gf_ref, o_ref, *, final_norm):
    x = x_ref[0]
    v = _rms(x, g_ref[...]).astype(BF16)
    h = _dot(v, w1_ref[...])
    h = jnp.square(jnp.maximum(h, 0.0)).astype(BF16)
    y = x + _dot(h, w2_ref[...])
    if final_norm:
        y = _rms(y, gf_ref[...])
    o_ref[0] = y


def _mlp(x, g, w1, w2, gf, *, tile, final_norm):
    S, L, D = x.shape
    assert L % tile == 0
    tok = pl.BlockSpec((1, tile, D), lambda s, t: (s, t, 0))
    return pl.pallas_call(
        functools.partial(_mlp_kernel, final_norm=final_norm),
        grid=(S, L // tile),
        in_specs=[tok, _whole((1, D)), _whole((D, D_FF)), _whole((D_FF, D)), _whole((1, D))],
        out_specs=tok,
        out_shape=jax.ShapeDtypeStruct((S, L, D), F32),
        compiler_params=_params(),
        name="mlp",
    )(x, g, w1, w2, gf)


def _pool_kernel(x_ref, hist_ref, g_ref, w_ref, sc_ref, o_ref, cache_ref, ext_ref, *, tile, p0):
    t = pl.program_id(1)

    @pl.when(t == 0)
    def _():
        ext_ref[0:POOL_HALO, :] = hist_ref[0]

    x = x_ref[0]
    u = _rms(x, g_ref[...])
    ext_ref[POOL_HALO:POOL_HALO + tile, :] = u

    pos = p0 + t * tile + lax.broadcasted_iota(jnp.int32, (tile, 1), 0)
    ys = []
    for gi, w in enumerate(POOL_WINDOWS):
        c0, c1 = gi * POOL_GC, (gi + 1) * POOL_GC
        win = u[:, c0:c1]
        for j in range(1, w):
            win = win + ext_ref[POOL_HALO - j:POOL_HALO - j + tile, c0:c1]
        cnt = jnp.minimum(pos + 1, w).astype(F32)
        diff = win / cnt - u[:, c0:c1]
        ys.append(_dot(diff.astype(BF16), w_ref[gi]))
    y = jnp.concatenate(ys, axis=1) * sc_ref[...]
    o_ref[0] = x + y

    tail = ext_ref[tile:tile + POOL_HALO, :]
    cache_ref[0] = tail
    ext_ref[0:POOL_HALO, :] = tail


def _pool(x, hist, g, w, sc, *, tile, p0):
    S, L, D = x.shape
    assert L % tile == 0 and tile >= POOL_HALO
    tok = pl.BlockSpec((1, tile, D), lambda s, t: (s, t, 0))
    seq = pl.BlockSpec((1, POOL_HALO, D), lambda s, t: (s, 0, 0))
    out, cache = pl.pallas_call(
        functools.partial(_pool_kernel, tile=tile, p0=p0),
        grid=(S, L // tile),
        in_specs=[tok, seq, _whole((1, D)), _whole((POOL_GROUPS, POOL_GC, POOL_GC)), _whole((1, D))],
        out_specs=[tok, seq],
        out_shape=[jax.ShapeDtypeStruct((S, L, D), F32), jax.ShapeDtypeStruct((S, POOL_HALO, D), F32)],
        scratch_shapes=[pltpu.VMEM((POOL_HALO + tile, D), F32)],
        compiler_params=_params(),
        name="pool",
    )(x, hist, g, w, sc)
    return out, cache[:, 1:, :]


def _build_attn_bias(rb_ref, bias_ref, *, tile):
    nb = ATTN_WINDOW // tile + 1
    w2 = 2 * tile
    col = lax.broadcasted_iota(jnp.int32, (REL_SIZE, nb * w2), 1)
    row = lax.broadcasted_iota(jnp.int32, (REL_SIZE, nb * w2), 0)
    ip = col % w2
    i = jnp.where(ip < tile, ip, ip - w2)
    idx = jnp.clip((col // w2) * tile - i, REL_MIN, REL_MAX) - REL_MIN
    sel = jnp.where(row == idx, 1.0, 0.0).astype(BF16)
    hi, mid, lo = _split3(rb_ref[...])
    gen = _dot(hi, sel) + _dot(mid, sel) + _dot(lo, sel)

    r = lax.broadcasted_iota(jnp.int32, (tile, w2), 0)
    lane = lax.broadcasted_iota(jnp.int32, (tile, w2), 1)
    first = lane < tile
    for d in range(nb):
        dchunk = (d * tile) // CHUNK + r // CHUNK - (lane % tile) // CHUNK
        ok = (dchunk >= 0) & (dchunk <= N_PAST_CHUNKS)
        for p in range(HEAD_PAIRS):
            ga = jnp.broadcast_to(gen[2 * p:2 * p + 1, d * w2:(d + 1) * w2], (tile, w2))
            gb = jnp.broadcast_to(gen[2 * p + 1:2 * p + 2, d * w2:(d + 1) * w2], (tile, w2))
            ta = pltpu.roll(ga, 0, 1, stride=1, stride_axis=0)
            tb = pltpu.roll(gb, tile, 1, stride=1, stride_axis=0)
            bias_ref[p, d] = jnp.where(ok, jnp.where(first, ta, tb), NEG_INF)


def _attn_kernel(x_ref, hk_ref, hv_ref, g_ref, wqkv_ref, wo_ref, rb_ref,
                 o_ref, ko_ref, vo_ref, kt_ring, v_ring, bias_ref, *, tile, mask_start):
    D = D_MODEL
    nb = ATTN_WINDOW // tile + 1
    t = pl.program_id(1)

    row_lo = lax.broadcasted_iota(jnp.int32, (PAIR_W, tile), 0) < ATTN_HD
    lane_lo = lax.broadcasted_iota(jnp.int32, (tile, PAIR_W), 1) < ATTN_HD
    first = lax.broadcasted_iota(jnp.int32, (tile, 2 * tile), 1) < tile

    @pl.when((pl.program_id(0) == 0) & (t == 0))
    def _():
        _build_attn_bias(rb_ref, bias_ref, tile=tile)

    def store_block(slot, k, v):
        kt = k.T.astype(BF16)
        vb = v.astype(BF16)
        zk = jnp.zeros((PAIR_W, tile), BF16)
        zv = jnp.zeros((tile, PAIR_W), BF16)
        for p in range(HEAD_PAIRS):
            ktp = kt[p * PAIR_W:(p + 1) * PAIR_W, :]
            kt_ring[slot, p] = jnp.concatenate(
                [jnp.where(row_lo, ktp, zk), jnp.where(row_lo, zk, ktp)], axis=1)
            vp = vb[:, p * PAIR_W:(p + 1) * PAIR_W]
            v_ring[slot, p] = jnp.concatenate(
                [jnp.where(lane_lo, vp, zv), jnp.where(lane_lo, zv, vp)], axis=0)

    @pl.when(t == 0)
    def _():
        for i in range(nb - 1):
            store_block(i + 1, hk_ref[0, i * tile:(i + 1) * tile, :], hv_ref[0, i * tile:(i + 1) * tile, :])

    x = x_ref[0]
    u = _rms(x, g_ref[...]).astype(BF16)
    qkv = _dot(u, wqkv_ref[...])
    q = (qkv[:, :D] * (ATTN_HD ** -0.5)).astype(BF16)
    k = qkv[:, D:2 * D]
    v = qkv[:, 2 * D:]
    ko_ref[0] = k
    vo_ref[0] = v
    store_block(lax.rem(t, nb), k, v)

    slots = [lax.rem(t + (nb - d), nb) for d in range(nb)]
    before_start = [jnp.where(t >= d, 0.0, NEG_INF) for d in range(nb)]

    def score_block(p, d):
        s = _dot(q[:, p * PAIR_W:(p + 1) * PAIR_W], kt_ring[slots[d], p]) + bias_ref[p, d]
        if mask_start and d > 0:
            s = s + before_start[d]
        return s

    def row_max(scores):
        mx = functools.reduce(jnp.maximum, scores)
        m_a = jnp.max(mx[:, :tile], axis=-1, keepdims=True)
        m_b = jnp.max(mx[:, tile:], axis=-1, keepdims=True)
        return jnp.where(first, m_a, m_b)

    outs = []
    nxt = [score_block(0, d) for d in range(nb)]
    for p in range(HEAD_PAIRS):
        scores, m = nxt, row_max(nxt)
        if p + 1 < HEAD_PAIRS:
            nxt = [score_block(p + 1, d) for d in range(nb)]
        lsum = jnp.zeros((tile, 2 * tile), F32)
        acc = jnp.zeros((tile, PAIR_W), F32)
        for d in range(nb):
            pr = jnp.exp(scores[d] - m)
            lsum = lsum + pr
            acc = acc + _dot(pr.astype(BF16), v_ring[slots[d], p])
        l_a = jnp.sum(lsum[:, :tile], axis=-1, keepdims=True)
        l_b = jnp.sum(lsum[:, tile:], axis=-1, keepdims=True)
        inv = jnp.where(lane_lo, 1.0 / l_a, 1.0 / l_b)
        outs.append((acc * inv).astype(BF16))
    o = jnp.concatenate(outs, axis=1)
    o_ref[0] = x + _dot(o, wo_ref[...])


def _attn(x, hist_k, hist_v, g, wqkv, wo, rel_bias, *, tile, keep, mask_start):
    S, L, D = x.shape
    nb = ATTN_WINDOW // tile + 1
    nt = L // tile
    nkeep = keep // tile
    assert L % tile == 0 and tile % CHUNK == 0 and ATTN_WINDOW % tile == 0 and keep % tile == 0
    tok = pl.BlockSpec((1, tile, D), lambda s, t: (s, t, 0))
    hist = pl.BlockSpec((1, ATTN_WINDOW, D), lambda s, t: (s, 0, 0))
    kv_out = pl.BlockSpec((1, tile, D), lambda s, t: (s, jnp.maximum(t - (nt - nkeep), 0), 0))
    return pl.pallas_call(
        functools.partial(_attn_kernel, tile=tile, mask_start=mask_start),
        grid=(S, nt),
        in_specs=[tok, hist, hist, _whole((1, D)), _whole((D, 3 * D)), _whole((D, D)),
                  _whole((ATTN_HEADS, REL_SIZE))],
        out_specs=[tok, kv_out, kv_out],
        out_shape=[jax.ShapeDtypeStruct((S, L, D), F32),
                   jax.ShapeDtypeStruct((S, keep, D), F32),
                   jax.ShapeDtypeStruct((S, keep, D), F32)],
        scratch_shapes=[pltpu.VMEM((nb, HEAD_PAIRS, PAIR_W, 2 * tile), BF16),
                        pltpu.VMEM((nb, HEAD_PAIRS, 2 * tile, PAIR_W), BF16),
                        pltpu.VMEM((HEAD_PAIRS, nb, tile, 2 * tile), F32)],
        compiler_params=_params(),
        name="attn",
    )(x, hist_k, hist_v, g, wqkv, wo, rel_bias)


def _tile_row(x, n, s):
    C, W = x.shape
    xr = x.reshape(C // n, n, W)
    return jnp.broadcast_to(xr[:, s:s + 1, :], (C // n, n, W)).reshape(C, W)


def _hgrn_intra_fast(q, kk, b2, vb, tril):
    C = q.shape[0]
    bmid = b2[C // 2 - 1:C // 2, :]
    qt = (q * jnp.exp2(b2 - bmid)).astype(BF16)
    kt = (kk * jnp.exp2(bmid - b2)).T.astype(BF16)
    amat = jnp.where(tril, _dot(qt, kt), 0.0)
    return _dot(amat.astype(BF16), vb)


def _hgrn_intra_safe(q, kk, b2, vv, vb, levels, rowid, ones):
    C = q.shape[0]
    amat = jnp.zeros((C, C), F32)
    for hl, upper, same in levels:
        bref = _tile_row(b2, 2 * hl, hl - 1)
        qt = q * jnp.exp2(jnp.where(upper, b2 - bref, NEG_INF))
        kt = kk * jnp.exp2(jnp.where(upper, NEG_INF, bref - b2))
        amat = amat + jnp.where(same, _dot(qt.astype(BF16), kt.T.astype(BF16)), 0.0)
    o = _dot(amat.astype(BF16), vb)
    sub = rowid % HG_DIAG
    for s in range(HG_DIAG):
        dec = jnp.exp2(jnp.where(sub >= s, b2 - _tile_row(b2, HG_DIAG, s), NEG_INF))
        ps = q * _tile_row(kk, HG_DIAG, s) * dec
        o = o + _dot(ps.astype(BF16), ones) * _tile_row(vv, HG_DIAG, s)
    return o


def _hgrn_kernel(x_ref, s0_ref, g_ref, win_ref, wo_ref, gn_ref, lbp_ref,
                 o_ref, so_ref, st_ref, pj_ref, b2_ref, oi_ref, og_ref, *, chunk, layer):
    D = D_MODEL
    C = chunk
    t = pl.program_id(1)
    nt = pl.num_programs(1)

    @pl.when(t == 0)
    def _():
        for h in range(HG_HEADS):
            st_ref[h] = s0_ref[0, h].T

    x = x_ref[0]
    u = _rms(x, g_ref[...]).astype(BF16)
    pj_ref[...] = _dot(u, win_ref[...])

    lbp = lbp_ref[...]
    e = jnp.exp(lbp - jnp.max(lbp, axis=0, keepdims=True))
    sm = e / jnp.sum(e, axis=0, keepdims=True)
    lb = jnp.zeros((1, D), F32)
    for i in range(1, layer + 1):
        lb = lb + sm[i:i + 1, :]
    log_lb = jnp.log(lb)
    log_1mlb = jnp.log1p(-lb)

    zf = pj_ref[:, D:2 * D]
    a = jnp.exp(-jnp.abs(zf))
    r = 1.0 / (1.0 + a)
    log_sig = jnp.minimum(zf, 0.0) - jnp.log(1.0 + a)
    bb = log_1mlb + log_sig
    logf = jnp.maximum(log_lb, bb) + jnp.log(1.0 + jnp.exp(-jnp.abs(log_lb - bb)))
    pj_ref[:, D:2 * D] = (1.0 - lb) * jnp.where(zf >= 0.0, a * r, r)
    zq = pj_ref[:, :D]
    pj_ref[:, :D] = zq * (1.0 / (1.0 + jnp.exp(-zq)))

    ri = lax.broadcasted_iota(jnp.int32, (C, C), 0)
    ci = lax.broadcasted_iota(jnp.int32, (C, C), 1)
    tril = ri >= ci
    tri = tril.astype(BF16)
    hi, mid, lo = _split3(logf)
    b2_all = (_dot(tri, hi) + _dot(tri, mid) + _dot(tri, lo)) * LOG2E
    b2_ref[...] = b2_all
    b2_mid = b2_all[C // 2 - 1:C // 2, :]
    b2_end = b2_all[C - 1:C, :]
    safe = (jnp.min(b2_mid) >= -HG_SAFE_LOG2) & (jnp.min(b2_end - b2_mid) >= -HG_SAFE_LOG2)

    def head(ref, part, h):
        return ref[:, part * D + h * HG_DK:part * D + (h + 1) * HG_DK]

    for h in range(HG_HEADS):
        q, kk, vv, b2 = head(pj_ref, 0, h), head(pj_ref, 1, h), head(pj_ref, 2, h), head(b2_ref, 0, h)
        st = st_ref[h]
        oi_ref[:, h * HG_DV:(h + 1) * HG_DV] = _dot((q * jnp.exp2(b2)).astype(BF16), st.T.astype(BF16))
        kdec = (kk * jnp.exp2(b2_end[:, h * HG_DK:(h + 1) * HG_DK] - b2)).astype(BF16)
        st_ref[h] = st * jnp.exp2(b2_end[:, h * HG_DK:(h + 1) * HG_DK]) + _dot_tn(vv.astype(BF16), kdec)

    @pl.when(safe)
    def _():
        for h in range(HG_HEADS):
            vv = head(pj_ref, 2, h)
            oi_ref[:, h * HG_DV:(h + 1) * HG_DV] += _hgrn_intra_fast(
                head(pj_ref, 0, h), head(pj_ref, 1, h), head(b2_ref, 0, h), vv.astype(BF16), tril)

    @pl.when(jnp.logical_not(safe))
    def _():
        rowid = lax.broadcasted_iota(jnp.int32, (C, 1), 0)
        ones = jnp.ones((HG_DK, HG_DV), BF16)
        levels = []
        hl = C // 2
        while hl >= HG_DIAG:
            levels.append((hl, (rowid % (2 * hl)) >= hl, (ri // (2 * hl)) == (ci // (2 * hl))))
            hl //= 2
        for h in range(HG_HEADS):
            vv = head(pj_ref, 2, h)
            oi_ref[:, h * HG_DV:(h + 1) * HG_DV] += _hgrn_intra_safe(
                head(pj_ref, 0, h), head(pj_ref, 1, h), head(b2_ref, 0, h), vv, vv.astype(BF16),
                levels, rowid, ones)

    for h in range(HG_HEADS):
        o = oi_ref[:, h * HG_DV:(h + 1) * HG_DV]
        zg = head(pj_ref, 3, h)
        on = o * lax.rsqrt(jnp.mean(o * o, axis=-1, keepdims=True) + EPS) * gn_ref[...]
        og_ref[:, h * HG_DV:(h + 1) * HG_DV] = (on * (zg * (1.0 / (1.0 + jnp.exp(-zg))))).astype(BF16)

    o_ref[0] = x + _dot(og_ref[...], wo_ref[...])

    @pl.when(t == nt - 1)
    def _():
        for h in range(HG_HEADS):
            so_ref[0, h] = st_ref[h].T


def _hgrn(x, s0, g, w_in, w_o, gn, lbp, *, chunk, layer):
    S, L, D = x.shape
    assert L % chunk == 0 and chunk % (2 * HG_DIAG) == 0
    tok = pl.BlockSpec((1, chunk, D), lambda s, t: (s, t, 0))
    state = pl.BlockSpec((1, HG_HEADS, HG_DK, HG_DV), lambda s, t: (s, 0, 0, 0))
    return pl.pallas_call(
        functools.partial(_hgrn_kernel, chunk=chunk, layer=layer),
        grid=(S, L // chunk),
        in_specs=[tok, state, _whole((1, D)), _whole((D, 4 * D)), _whole((D, D)), _whole((1, HG_DV)),
                  _whole((DEPTH, D))],
        out_specs=[tok, state],
        out_shape=[jax.ShapeDtypeStruct((S, L, D), F32),
                   jax.ShapeDtypeStruct((S, HG_HEADS, HG_DK, HG_DV), F32)],
        scratch_shapes=[pltpu.VMEM((HG_HEADS, HG_DV, HG_DK), F32),
                        pltpu.VMEM((chunk, 4 * D), F32),
                        pltpu.VMEM((chunk, D), F32),
                        pltpu.VMEM((chunk, D), F32),
                        pltpu.VMEM((chunk, D), BF16)],
        compiler_params=_params(),
        name="hgrn",
    )(x, s0, g, w_in, w_o, gn, lbp)


MLP_TILE = (512, 512)
POOL_TILE = (512, 64)
ATTN_TILE = (128, 64)
HGRN_CHUNK = (128, 64)


def kernel(x_prompt, x_sample, cache_pool, cache_attn_k, cache_attn_v, state_hgrn, norm_mix, norm_ffn,
           norm_final, pool_w, pool_scale, attn_wqkv, attn_wo, attn_rel_bias, hgrn_w_in, hgrn_w_o,
           hgrn_gnorm, hgrn_lower_bounds, mlp_w1, mlp_w2):
    B, L, D = x_prompt.shape
    BS, LS, _ = x_sample.shape
    xp = x_prompt
    xs = x_sample.reshape(1, BS * LS, D)
    gf = norm_final.reshape(1, D)

    pool_p, pool_s, k_p, k_s, v_p, v_s, h_p, h_s = [], [], [], [], [], [], [], []
    for i in range(DEPTH):
        kind, j = i % N_MIXERS, i // N_MIXERS
        g = norm_mix[i].reshape(1, D)
        xs = xs.reshape(BS, LS, D)
        if kind == 0:
            w = pool_w[j].astype(BF16)
            sc = pool_scale[j].reshape(1, D)
            hist_p = jnp.zeros((B, POOL_HALO, D), F32)
            hist_s = jnp.pad(cache_pool[j], ((0, 0), (1, 0), (0, 0)))
            xp, cp = _pool(xp, hist_p, g, w, sc, tile=POOL_TILE[0], p0=0)
            xs, cs = _pool(xs, hist_s, g, w, sc, tile=POOL_TILE[1], p0=PAST_LEN)
            pool_p.append(cp)
            pool_s.append(cs)
        elif kind == 1:
            wqkv = attn_wqkv[j].astype(BF16)
            wo = attn_wo[j].astype(BF16)
            keep_p = min(ATTN_WINDOW, L)
            zeros = jnp.zeros((B, ATTN_WINDOW, D), F32)
            xp, kp, vp = _attn(xp, zeros, zeros, g, wqkv, wo, attn_rel_bias[j],
                               tile=ATTN_TILE[0], keep=keep_p, mask_start=True)
            ck = cache_attn_k[j].reshape(BS, ATTN_WINDOW, D)
            cv = cache_attn_v[j].reshape(BS, ATTN_WINDOW, D)
            xs, ks, vs = _attn(xs, ck, cv, g, wqkv, wo, attn_rel_bias[j],
                               tile=ATTN_TILE[1], keep=LS, mask_start=False)
            k_p.append(kp.reshape(B, keep_p, ATTN_HEADS, ATTN_HD))
            v_p.append(vp.reshape(B, keep_p, ATTN_HEADS, ATTN_HD))
            k_s.append(ks.reshape(BS, LS, ATTN_HEADS, ATTN_HD))
            v_s.append(vs.reshape(BS, LS, ATTN_HEADS, ATTN_HD))
        else:
            w_in = hgrn_w_in[j].astype(BF16)
            w_o = hgrn_w_o[j].astype(BF16)
            gn = hgrn_gnorm[j].reshape(1, HG_DV)
            s0 = jnp.zeros((B, HG_HEADS, HG_DK, HG_DV), F32)
            xp, sp = _hgrn(xp, s0, g, w_in, w_o, gn, hgrn_lower_bounds, chunk=HGRN_CHUNK[0], layer=i)
            xs, ss = _hgrn(xs, state_hgrn[j], g, w_in, w_o, gn, hgrn_lower_bounds, chunk=HGRN_CHUNK[1],
                           layer=i)
            h_p.append(sp)
            h_s.append(ss)
        gm = norm_ffn[i].reshape(1, D)
        w1 = mlp_w1[i].astype(BF16)
        w2 = mlp_w2[i].astype(BF16)
        last = i == DEPTH - 1
        xp = _mlp(xp, gm, w1, w2, gf, tile=MLP_TILE[0], final_norm=last)
        xs = _mlp(xs.reshape(1, BS * LS, D), gm, w1, w2, gf, tile=MLP_TILE[1], final_norm=last)

    return (xp, xs.reshape(BS, LS, D), jnp.stack(pool_p), jnp.stack(pool_s), jnp.stack(k_p), jnp.stack(k_s),
            jnp.stack(v_p), jnp.stack(v_s), jnp.stack(h_p), jnp.stack(h_s))
```

```python
import functools
import math

import jax
import jax.numpy as jnp
from jax import lax
from jax.experimental import pallas as pl
from jax.experimental.pallas import tpu as pltpu

F32 = jnp.float32
BF16 = jnp.bfloat16

D_MODEL = 1024
DEPTH = 4
CHUNK = 64
N_MIXERS = 3
PAST_LEN = 2048
EPS = 1e-6
POOL_WINDOWS = (2, 4, 8, 16)
POOL_GROUPS = 4
POOL_GC = D_MODEL // POOL_GROUPS
POOL_HIST = max(POOL_WINDOWS) - 1
POOL_HALO = POOL_HIST + 1
ATTN_HEADS = 16
ATTN_HD = D_MODEL // ATTN_HEADS
N_PAST_CHUNKS = 8
ATTN_WINDOW = N_PAST_CHUNKS * CHUNK
REL_MIN = -(CHUNK - 1)
REL_MAX = 256
REL_SIZE = REL_MAX - REL_MIN + 1
HEAD_PAIRS = ATTN_HEADS // 2
PAIR_W = 2 * ATTN_HD
HG_HEADS = 8
HG_DK = D_MODEL // HG_HEADS
HG_DV = D_MODEL // HG_HEADS
HG_DIAG = 8
HG_SAFE_LOG2 = 100.0
LOG2E = math.log2(math.e)
D_FF = 4 * D_MODEL

V7X_VMEM_BYTES = 64 * 1024 * 1024
VMEM_LIMIT_BYTES = V7X_VMEM_BYTES - 8 * 1024 * 1024

NEG_INF = float("-inf")


def _params():
    return pltpu.CompilerParams(dimension_semantics=("arbitrary", "arbitrary"),
                                vmem_limit_bytes=VMEM_LIMIT_BYTES)


def _whole(shape):
    zeros = (0,) * len(shape)
    return pl.BlockSpec(shape, lambda s, t: zeros)


def _layer(shape, i, single_buffer=False):
    zeros = (0,) * len(shape)
    kw = dict(pipeline_mode=pl.Buffered(1)) if single_buffer else {}
    return pl.BlockSpec((None,) + tuple(shape), lambda s, t: (i,) + zeros, **kw)


def _rms(x, g):
    ms = jnp.mean(x * x, axis=-1, keepdims=True)
    return x * lax.rsqrt(ms + EPS) * g


def _dot(a, b):
    return jnp.dot(a, b, preferred_element_type=F32)


def _dot_tn(a, b):
    return lax.dot_general(a, b, (((0,), (0,)), ((), ())), preferred_element_type=F32)


def _split3(x):
    hi = x.astype(BF16)
    r1 = x - hi.astype(F32)
    mid = r1.astype(BF16)
    lo = (r1 - mid.astype(F32)).astype(BF16)
    return hi, mid, lo


def _mlp_apply(x, g, w1_ref, w2_ref, gf, final_norm):
    v = _rms(x, g).astype(BF16)
    h = _dot(v, w1_ref[...])
    h = jnp.square(jnp.maximum(h, 0.0)).astype(BF16)
    y = x + _dot(h, w2_ref[...])
    if final_norm:
        y = _rms(y, gf)
    return y


def _mlp_kernel(x_ref, g_ref, w1_ref, w2_ref, gf_ref, o_ref, *, final_norm):
    o_ref[0] = _mlp_apply(x_ref[0], g_ref[...], w1_ref, w2_ref, gf_ref[...], final_norm)


def _mlp(x, g, w1, w2, gf, *, layer, tile, final_norm):
    S, L, D = x.shape
    assert L % tile == 0
    tok = pl.BlockSpec((1, tile, D), lambda s, t: (s, t, 0))
    return pl.pallas_call(
        functools.partial(_mlp_kernel, final_norm=final_norm),
        grid=(S, L // tile),
        in_specs=[tok, _whole((1, D)), _layer((D, D_FF), layer), _layer((D_FF, D), layer), _whole((1, D))],
        out_specs=tok,
        out_shape=jax.ShapeDtypeStruct((S, L, D), F32),
        compiler_params=_params(),
        name="mlp",
    )(x, g, w1, w2, gf)


def _pool_group(u, pos, ext_ref, w_ref, gi, tile):
    w = POOL_WINDOWS[gi]
    c0, c1 = gi * POOL_GC, (gi + 1) * POOL_GC
    win = u[:, c0:c1]
    for j in range(1, w):
        win = win + ext_ref[POOL_HALO - j:POOL_HALO - j + tile, c0:c1]
    cnt = jnp.minimum(pos + 1, w).astype(F32)
    diff = win / cnt - u[:, c0:c1]
    return _dot(diff.astype(BF16), w_ref[gi])


def _pool_apply(x, pos0, ext_ref, g, w_ref, sc, tile):
    u = _rms(x, g)
    ext_ref[POOL_HALO:POOL_HALO + tile, :] = u
    pos = pos0 + lax.broadcasted_iota(jnp.int32, (tile, 1), 0)
    ys = [_pool_group(u, pos, ext_ref, w_ref, gi, tile) for gi in range(POOL_GROUPS)]
    y = jnp.concatenate(ys, axis=1) * sc
    return x + y, ext_ref[tile:tile + POOL_HALO, :]


def _pool_kernel(x_ref, hist_ref, g_ref, w_ref, sc_ref, o_ref, cache_ref, ext_ref, *, tile, p0):
    t = pl.program_id(1)

    @pl.when(t == 0)
    def _():
        ext_ref[0:POOL_HALO, :] = hist_ref[0]

    out, tail = _pool_apply(x_ref[0], p0 + t * tile, ext_ref, g_ref[...], w_ref, sc_ref[...], tile)
    o_ref[0] = out
    cache_ref[0] = tail
    ext_ref[0:POOL_HALO, :] = tail


def _pool_mlp_kernel(x0_ref, xn_ref, hist_ref, gm_ref, pw_ref, sc_ref, gffn_ref, w1_ref, w2_ref, gf_ref,
                     o_ref, cache_ref, ext_ref, x1_ref, *, tile, p0, final_norm):
    t = pl.program_id(1)
    nt = pl.num_programs(1)

    @pl.when(t == 0)
    def _():
        ext_ref[0:POOL_HALO, :] = hist_ref[0]
        out, tail = _pool_apply(x0_ref[0], p0, ext_ref, gm_ref[...], pw_ref, sc_ref[...], tile)
        x1_ref[0] = out
        ext_ref[0:POOL_HALO, :] = tail

    x1 = x1_ref[lax.rem(t, 2)]
    v = _rms(x1, gffn_ref[...]).astype(BF16)
    halo = ext_ref[0:POOL_HALO, :]
    xn = xn_ref[0]
    u = _rms(xn, gm_ref[...])
    ext_ref[POOL_HALO:POOL_HALO + tile, :] = u
    pos = p0 + (t + 1) * tile + lax.broadcasted_iota(jnp.int32, (tile, 1), 0)
    fc = D_FF // POOL_GROUPS
    acc = x1
    ys = []
    for c in range(POOL_GROUPS):
        h = _dot(v, w1_ref[:, c * fc:(c + 1) * fc])
        h = jnp.square(jnp.maximum(h, 0.0)).astype(BF16)
        acc = acc + _dot(h, w2_ref[c * fc:(c + 1) * fc, :])
        ys.append(_pool_group(u, pos, ext_ref, pw_ref, c, tile))
    if final_norm:
        acc = _rms(acc, gf_ref[...])
    o_ref[0] = acc

    x1_ref[lax.rem(t + 1, 2)] = xn + jnp.concatenate(ys, axis=1) * sc_ref[...]
    halo = jnp.where(t + 1 < nt, ext_ref[tile:tile + POOL_HALO, :], halo)
    ext_ref[0:POOL_HALO, :] = halo
    cache_ref[0] = halo


def _pool_mlp(x, hist, gm, pw, sc, gffn, w1, w2, gf, *, pool_layer, layer, tile, p0, final_norm):
    S, L, D = x.shape
    nt = L // tile
    assert L % tile == 0 and tile >= POOL_HALO
    first = pl.BlockSpec((1, tile, D), lambda s, t: (s, 0, 0))
    nxt = pl.BlockSpec((1, tile, D), lambda s, t: (s, jnp.minimum(t + 1, nt - 1), 0))
    tok = pl.BlockSpec((1, tile, D), lambda s, t: (s, t, 0))
    seq = pl.BlockSpec((1, POOL_HALO, D), lambda s, t: (s, 0, 0))
    out, cache = pl.pallas_call(
        functools.partial(_pool_mlp_kernel, tile=tile, p0=p0, final_norm=final_norm),
        grid=(S, nt),
        in_specs=[first, nxt, seq, _whole((1, D)), _layer((POOL_GROUPS, POOL_GC, POOL_GC), pool_layer),
                  _whole((1, D)), _whole((1, D)),
                  _layer((D, D_FF), layer, single_buffer=True), _layer((D_FF, D), layer, single_buffer=True),
                  _whole((1, D))],
        out_specs=[tok, seq],
        out_shape=[jax.ShapeDtypeStruct((S, L, D), F32), jax.ShapeDtypeStruct((S, POOL_HALO, D), F32)],
        scratch_shapes=[pltpu.VMEM((POOL_HALO + tile, D), F32), pltpu.VMEM((2, tile, D), F32)],
        compiler_params=_params(),
        name="pool_mlp",
    )(x, x, hist, gm, pw, sc, gffn, w1, w2, gf)
    return out, cache[:, 1:, :]


def _pool(x, hist, g, w, sc, *, pool_layer, tile, p0):
    S, L, D = x.shape
    assert L % tile == 0 and tile >= POOL_HALO
    tok = pl.BlockSpec((1, tile, D), lambda s, t: (s, t, 0))
    seq = pl.BlockSpec((1, POOL_HALO, D), lambda s, t: (s, 0, 0))
    out, cache = pl.pallas_call(
        functools.partial(_pool_kernel, tile=tile, p0=p0),
        grid=(S, L // tile),
        in_specs=[tok, seq, _whole((1, D)), _layer((POOL_GROUPS, POOL_GC, POOL_GC), pool_layer), _whole((1, D))],
        out_specs=[tok, seq],
        out_shape=[jax.ShapeDtypeStruct((S, L, D), F32), jax.ShapeDtypeStruct((S, POOL_HALO, D), F32)],
        scratch_shapes=[pltpu.VMEM((POOL_HALO + tile, D), F32)],
        compiler_params=_params(),
        name="pool",
    )(x, hist, g, w, sc)
    return out, cache[:, 1:, :]


def _build_attn_bias(rb_ref, bias_ref, *, tile):
    nb = ATTN_WINDOW // tile + 1
    w2 = 2 * tile
    col = lax.broadcasted_iota(jnp.int32, (REL_SIZE, nb * w2), 1)
    row = lax.broadcasted_iota(jnp.int32, (REL_SIZE, nb * w2), 0)
    ip = col % w2
    i = jnp.where(ip < tile, ip, ip - w2)
    idx = jnp.clip((col // w2) * tile - i, REL_MIN, REL_MAX) - REL_MIN
    sel = jnp.where(row == idx, 1.0, 0.0).astype(BF16)
    hi, mid, lo = _split3(rb_ref[...])
    gen = _dot(hi, sel) + _dot(mid, sel) + _dot(lo, sel)

    r = lax.broadcasted_iota(jnp.int32, (tile, w2), 0)
    lane = lax.broadcasted_iota(jnp.int32, (tile, w2), 1)
    first = lane < tile
    for d in range(nb):
        dchunk = (d * tile) // CHUNK + r // CHUNK - (lane % tile) // CHUNK
        ok = (dchunk >= 0) & (dchunk <= N_PAST_CHUNKS)
        for p in range(HEAD_PAIRS):
            ga = jnp.broadcast_to(gen[2 * p:2 * p + 1, d * w2:(d + 1) * w2], (tile, w2))
            gb = jnp.broadcast_to(gen[2 * p + 1:2 * p + 2, d * w2:(d + 1) * w2], (tile, w2))
            ta = pltpu.roll(ga, 0, 1, stride=1, stride_axis=0)
            tb = pltpu.roll(gb, tile, 1, stride=1, stride_axis=0)
            bias_ref[p, d] = jnp.where(ok, jnp.where(first, ta, tb), NEG_INF)


def _attn_kernel(x_ref, hk_ref, hv_ref, g_ref, wqkv_ref, wo_ref, rb_ref,
                 o_ref, ko_ref, vo_ref, kt_ring, v_ring, bias_ref, *, tile, mask_start):
    D = D_MODEL
    nb = ATTN_WINDOW // tile + 1
    t = pl.program_id(1)

    row_lo = lax.broadcasted_iota(jnp.int32, (PAIR_W, tile), 0) < ATTN_HD
    lane_lo = lax.broadcasted_iota(jnp.int32, (tile, PAIR_W), 1) < ATTN_HD
    first = lax.broadcasted_iota(jnp.int32, (tile, 2 * tile), 1) < tile

    @pl.when((pl.program_id(0) == 0) & (t == 0))
    def _():
        _build_attn_bias(rb_ref, bias_ref, tile=tile)

    def store_block(slot, k, v):
        kt = k.T.astype(BF16)
        vb = v.astype(BF16)
        zk = jnp.zeros((PAIR_W, tile), BF16)
        zv = jnp.zeros((tile, PAIR_W), BF16)
        for p in range(HEAD_PAIRS):
            ktp = kt[p * PAIR_W:(p + 1) * PAIR_W, :]
            kt_ring[slot, p] = jnp.concatenate(
                [jnp.where(row_lo, ktp, zk), jnp.where(row_lo, zk, ktp)], axis=1)
            vp = vb[:, p * PAIR_W:(p + 1) * PAIR_W]
            v_ring[slot, p] = jnp.concatenate(
                [jnp.where(lane_lo, vp, zv), jnp.where(lane_lo, zv, vp)], axis=0)

    @pl.when(t == 0)
    def _():
        for i in range(nb - 1):
            store_block(i + 1, hk_ref[0, i * tile:(i + 1) * tile, :], hv_ref[0, i * tile:(i + 1) * tile, :])

    x = x_ref[0]
    u = _rms(x, g_ref[...]).astype(BF16)
    qkv = _dot(u, wqkv_ref[...])
    q = (qkv[:, :D] * (ATTN_HD ** -0.5)).astype(BF16)
    k = qkv[:, D:2 * D]
    v = qkv[:, 2 * D:]
    ko_ref[0] = k
    vo_ref[0] = v
    store_block(lax.rem(t, nb), k, v)

    slots = [lax.rem(t + (nb - d), nb) for d in range(nb)]
    before_start = [jnp.where(t >= d, 0.0, NEG_INF) for d in range(nb)]

    def score_block(p, d):
        s = _dot(q[:, p * PAIR_W:(p + 1) * PAIR_W], kt_ring[slots[d], p]) + bias_ref[p, d]
        if mask_start and d > 0:
            s = s + before_start[d]
        return s

    def row_max(scores):
        mx = functools.reduce(jnp.maximum, scores)
        m_a = jnp.max(mx[:, :tile], axis=-1, keepdims=True)
        m_b = jnp.max(mx[:, tile:], axis=-1, keepdims=True)
        return jnp.where(first, m_a, m_b)

    outs = []
    nxt = [score_block(0, d) for d in range(nb)]
    for p in range(HEAD_PAIRS):
        scores, m = nxt, row_max(nxt)
        if p + 1 < HEAD_PAIRS:
            nxt = [score_block(p + 1, d) for d in range(nb)]
        lsum = jnp.zeros((tile, 2 * tile), F32)
        acc = jnp.zeros((tile, PAIR_W), F32)
        for d in range(nb):
            pr = jnp.exp(scores[d] - m)
            lsum = lsum + pr
            acc = acc + _dot(pr.astype(BF16), v_ring[slots[d], p])
        l_a = jnp.sum(lsum[:, :tile], axis=-1, keepdims=True)
        l_b = jnp.sum(lsum[:, tile:], axis=-1, keepdims=True)
        inv = jnp.where(lane_lo, 1.0 / l_a, 1.0 / l_b)
        outs.append((acc * inv).astype(BF16))
    o = jnp.concatenate(outs, axis=1)
    o_ref[0] = x + _dot(o, wo_ref[...])


def _attn(x, hist_k, hist_v, g, wqkv, wo, rel_bias, *, tile, keep, mask_start):
    S, L, D = x.shape
    nb = ATTN_WINDOW // tile + 1
    nt = L // tile
    nkeep = keep // tile
    assert L % tile == 0 and tile % CHUNK == 0 and ATTN_WINDOW % tile == 0 and keep % tile == 0
    tok = pl.BlockSpec((1, tile, D), lambda s, t: (s, t, 0))
    hist = pl.BlockSpec((1, ATTN_WINDOW, D), lambda s, t: (s, 0, 0))
    kv_out = pl.BlockSpec((1, tile, D), lambda s, t: (s, jnp.maximum(t - (nt - nkeep), 0), 0))
    return pl.pallas_call(
        functools.partial(_attn_kernel, tile=tile, mask_start=mask_start),
        grid=(S, nt),
        in_specs=[tok, hist, hist, _whole((1, D)), _whole((D, 3 * D)), _whole((D, D)),
                  _whole((ATTN_HEADS, REL_SIZE))],
        out_specs=[tok, kv_out, kv_out],
        out_shape=[jax.ShapeDtypeStruct((S, L, D), F32),
                   jax.ShapeDtypeStruct((S, keep, D), F32),
                   jax.ShapeDtypeStruct((S, keep, D), F32)],
        scratch_shapes=[pltpu.VMEM((nb, HEAD_PAIRS, PAIR_W, 2 * tile), BF16),
                        pltpu.VMEM((nb, HEAD_PAIRS, 2 * tile, PAIR_W), BF16),
                        pltpu.VMEM((HEAD_PAIRS, nb, tile, 2 * tile), F32)],
        compiler_params=_params(),
        name="attn",
    )(x, hist_k, hist_v, g, wqkv, wo, rel_bias)


def _tile_row(x, n, s):
    C, W = x.shape
    xr = x.reshape(C // n, n, W)
    return jnp.broadcast_to(xr[:, s:s + 1, :], (C // n, n, W)).reshape(C, W)


def _hgrn_intra_fast(q, kk, b2, vb, tril):
    C = q.shape[0]
    bmid = b2[C // 2 - 1:C // 2, :]
    qt = (q * jnp.exp2(b2 - bmid)).astype(BF16)
    kt = (kk * jnp.exp2(bmid - b2)).T.astype(BF16)
    amat = jnp.where(tril, _dot(qt, kt), 0.0)
    return _dot(amat.astype(BF16), vb)


def _hgrn_intra_safe(q, kk, b2, vv, vb, levels, rowid, ones):
    C = q.shape[0]
    amat = jnp.zeros((C, C), F32)
    for hl, upper, same in levels:
        bref = _tile_row(b2, 2 * hl, hl - 1)
        qt = q * jnp.exp2(jnp.where(upper, b2 - bref, NEG_INF))
        kt = kk * jnp.exp2(jnp.where(upper, NEG_INF, bref - b2))
        amat = amat + jnp.where(same, _dot(qt.astype(BF16), kt.T.astype(BF16)), 0.0)
    o = _dot(amat.astype(BF16), vb)
    sub = rowid % HG_DIAG
    for s in range(HG_DIAG):
        dec = jnp.exp2(jnp.where(sub >= s, b2 - _tile_row(b2, HG_DIAG, s), NEG_INF))
        ps = q * _tile_row(kk, HG_DIAG, s) * dec
        o = o + _dot(ps.astype(BF16), ones) * _tile_row(vv, HG_DIAG, s)
    return o


def _hgrn_kernel(x_ref, s0_ref, g_ref, win_ref, wo_ref, gn_ref, lbp_ref,
                 o_ref, so_ref, st_ref, pj_ref, b2_ref, og_ref, *, chunk, layer):
    D = D_MODEL
    C = chunk
    t = pl.program_id(1)
    nt = pl.num_programs(1)

    @pl.when(t == 0)
    def _():
        for h in range(HG_HEADS):
            st_ref[h] = s0_ref[0, h].T

    x = x_ref[0]
    u = _rms(x, g_ref[...]).astype(BF16)
    pj_ref[...] = _dot(u, win_ref[...])

    lbp = lbp_ref[...]
    e = jnp.exp(lbp - jnp.max(lbp, axis=0, keepdims=True))
    sm = e / jnp.sum(e, axis=0, keepdims=True)
    lb = jnp.zeros((1, D), F32)
    for i in range(1, layer + 1):
        lb = lb + sm[i:i + 1, :]
    log_lb = jnp.log(lb)
    log_1mlb = jnp.log1p(-lb)

    zf = pj_ref[:, D:2 * D]
    a = jnp.exp(-jnp.abs(zf))
    r = 1.0 / (1.0 + a)
    log_sig = jnp.minimum(zf, 0.0) - jnp.log(1.0 + a)
    bb = log_1mlb + log_sig
    logf = jnp.maximum(log_lb, bb) + jnp.log(1.0 + jnp.exp(-jnp.abs(log_lb - bb)))
    pj_ref[:, D:2 * D] = (1.0 - lb) * jnp.where(zf >= 0.0, a * r, r)
    zq = pj_ref[:, :D]
    pj_ref[:, :D] = zq * (1.0 / (1.0 + jnp.exp(-zq)))

    ri = lax.broadcasted_iota(jnp.int32, (C, C), 0)
    ci = lax.broadcasted_iota(jnp.int32, (C, C), 1)
    tril = ri >= ci
    tri = tril.astype(BF16)
    hi, mid, lo = _split3(logf)
    b2_all = (_dot(tri, hi) + _dot(tri, mid) + _dot(tri, lo)) * LOG2E
    b2_ref[...] = b2_all
    b2_mid = b2_all[C // 2 - 1:C // 2, :]
    b2_end = b2_all[C - 1:C, :]
    safe = (jnp.min(b2_mid) >= -HG_SAFE_LOG2) & (jnp.min(b2_end - b2_mid) >= -HG_SAFE_LOG2)

    def head(ref, part, h):
        return ref[:, part * D + h * HG_DK:part * D + (h + 1) * HG_DK]

    def finish(intra):
        for h in range(HG_HEADS):
            q, kk, vv, b2 = head(pj_ref, 0, h), head(pj_ref, 1, h), head(pj_ref, 2, h), head(b2_ref, 0, h)
            zg = head(pj_ref, 3, h)
            vb = vv.astype(BF16)
            st = st_ref[h]
            end = b2_end[:, h * HG_DK:(h + 1) * HG_DK]
            o = _dot((q * jnp.exp2(b2)).astype(BF16), st.T.astype(BF16))
            kdec = (kk * jnp.exp2(end - b2)).astype(BF16)
            st_ref[h] = st * jnp.exp2(end) + _dot_tn(vb, kdec)
            o = o + intra(q, kk, b2, vv, vb)
            on = o * lax.rsqrt(jnp.mean(o * o, axis=-1, keepdims=True) + EPS) * gn_ref[...]
            og_ref[:, h * HG_DV:(h + 1) * HG_DV] = (on * (zg * (1.0 / (1.0 + jnp.exp(-zg))))).astype(BF16)
        o_ref[0] = x + _dot(og_ref[...], wo_ref[...])

    @pl.when(safe)
    def _():
        finish(lambda q, kk, b2, vv, vb: _hgrn_intra_fast(q, kk, b2, vb, tril))

    @pl.when(jnp.logical_not(safe))
    def _():
        rowid = lax.broadcasted_iota(jnp.int32, (C, 1), 0)
        ones = jnp.ones((HG_DK, HG_DV), BF16)
        levels = []
        hl = C // 2
        while hl >= HG_DIAG:
            levels.append((hl, (rowid % (2 * hl)) >= hl, (ri // (2 * hl)) == (ci // (2 * hl))))
            hl //= 2
        finish(lambda q, kk, b2, vv, vb: _hgrn_intra_safe(q, kk, b2, vv, vb, levels, rowid, ones))

    @pl.when(t == nt - 1)
    def _():
        for h in range(HG_HEADS):
            so_ref[0, h] = st_ref[h].T


def _hgrn(x, s0, g, w_in, w_o, gn, lbp, *, chunk, layer):
    S, L, D = x.shape
    assert L % chunk == 0 and chunk % (2 * HG_DIAG) == 0
    tok = pl.BlockSpec((1, chunk, D), lambda s, t: (s, t, 0))
    state = pl.BlockSpec((1, HG_HEADS, HG_DK, HG_DV), lambda s, t: (s, 0, 0, 0))
    return pl.pallas_call(
        functools.partial(_hgrn_kernel, chunk=chunk, layer=layer),
        grid=(S, L // chunk),
        in_specs=[tok, state, _whole((1, D)), _whole((D, 4 * D)), _whole((D, D)), _whole((1, HG_DV)),
                  _whole((DEPTH, D))],
        out_specs=[tok, state],
        out_shape=[jax.ShapeDtypeStruct((S, L, D), F32),
                   jax.ShapeDtypeStruct((S, HG_HEADS, HG_DK, HG_DV), F32)],
        scratch_shapes=[pltpu.VMEM((HG_HEADS, HG_DV, HG_DK), F32),
                        pltpu.VMEM((chunk, 4 * D), F32),
                        pltpu.VMEM((chunk, D), F32),
                        pltpu.VMEM((chunk, D), BF16)],
        compiler_params=_params(),
        name="hgrn",
    )(x, s0, g, w_in, w_o, gn, lbp)


MLP_TILE = (512, 512)
POOL_TILE = (512, 64)
ATTN_TILE = (128, 64)
HGRN_CHUNK = (128, 64)


def kernel(x_prompt, x_sample, cache_pool, cache_attn_k, cache_attn_v, state_hgrn, norm_mix, norm_ffn,
           norm_final, pool_w, pool_scale, attn_wqkv, attn_wo, attn_rel_bias, hgrn_w_in, hgrn_w_o,
           hgrn_gnorm, hgrn_lower_bounds, mlp_w1, mlp_w2):
    B, L, D = x_prompt.shape
    BS, LS, _ = x_sample.shape
    xp = x_prompt
    xs = x_sample
    gf = norm_final.reshape(1, D)
    w1_all = mlp_w1.astype(BF16)
    w2_all = mlp_w2.astype(BF16)
    pw_all = pool_w.astype(BF16)

    pool_p, pool_s, k_p, k_s, v_p, v_s, h_p, h_s = [], [], [], [], [], [], [], []
    for i in range(DEPTH):
        kind, j = i % N_MIXERS, i // N_MIXERS
        g = norm_mix[i].reshape(1, D)
        gm = norm_ffn[i].reshape(1, D)
        last = i == DEPTH - 1
        if kind == 0:
            sc = pool_scale[j].reshape(1, D)
            hist_p = jnp.zeros((B, POOL_HALO, D), F32)
            hist_s = jnp.pad(cache_pool[j], ((0, 0), (1, 0), (0, 0)))
            xp, cp = _pool_mlp(xp, hist_p, g, pw_all, sc, gm, w1_all, w2_all, gf, pool_layer=j, layer=i,
                               tile=MLP_TILE[0], p0=0, final_norm=last)
            xs, cs = _pool(xs, hist_s, g, pw_all, sc, pool_layer=j, tile=POOL_TILE[1], p0=PAST_LEN)
            pool_p.append(cp)
            pool_s.append(cs)
        elif kind == 1:
            wqkv = attn_wqkv[j].astype(BF16)
            wo = attn_wo[j].astype(BF16)
            keep_p = min(ATTN_WINDOW, L)
            zeros = jnp.zeros((B, ATTN_WINDOW, D), F32)
            xp, kp, vp = _attn(xp, zeros, zeros, g, wqkv, wo, attn_rel_bias[j],
                               tile=ATTN_TILE[0], keep=keep_p, mask_start=True)
            ck = cache_attn_k[j].reshape(BS, ATTN_WINDOW, D)
            cv = cache_attn_v[j].reshape(BS, ATTN_WINDOW, D)
            xs, ks, vs = _attn(xs, ck, cv, g, wqkv, wo, attn_rel_bias[j],
                               tile=ATTN_TILE[1], keep=LS, mask_start=False)
            k_p.append(kp.reshape(B, keep_p, ATTN_HEADS, ATTN_HD))
            v_p.append(vp.reshape(B, keep_p, ATTN_HEADS, ATTN_HD))
            k_s.append(ks.reshape(BS, LS, ATTN_HEADS, ATTN_HD))
            v_s.append(vs.reshape(BS, LS, ATTN_HEADS, ATTN_HD))
        else:
            w_in = hgrn_w_in[j].astype(BF16)
            w_o = hgrn_w_o[j].astype(BF16)
            gn = hgrn_gnorm[j].reshape(1, HG_DV)
            s0 = jnp.zeros((B, HG_HEADS, HG_DK, HG_DV), F32)
            xp, sp = _hgrn(xp, s0, g, w_in, w_o, gn, hgrn_lower_bounds, chunk=HGRN_CHUNK[0], layer=i)
            xs, ss = _hgrn(xs, state_hgrn[j], g, w_in, w_o, gn, hgrn_lower_bounds, chunk=HGRN_CHUNK[1],
                           layer=i)
            h_p.append(sp)
            h_s.append(ss)
        if kind != 0:
            xp = _mlp(xp, gm, w1_all, w2_all, gf, layer=i, tile=MLP_TILE[0], final_norm=last)
        xs = _mlp(xs.reshape(1, BS * LS, D), gm, w1_all, w2_all, gf, layer=i, tile=MLP_TILE[1],
                  final_norm=last).reshape(BS, LS, D)

    return (xp, xs, jnp.stack(pool_p), jnp.stack(pool_s), jnp.stack(k_p), jnp.stack(k_s),
            jnp.stack(v_p), jnp.stack(v_s), jnp.stack(h_p), jnp.stack(h_s))
```

```python
import functools
import math

import jax
import jax.numpy as jnp
from jax import lax
from jax.experimental import pallas as pl
from jax.experimental.pallas import tpu as pltpu

F32 = jnp.float32
BF16 = jnp.bfloat16

D_MODEL = 1024
DEPTH = 4
CHUNK = 64
N_MIXERS = 3
PAST_LEN = 2048
EPS = 1e-6
POOL_WINDOWS = (2, 4, 8, 16)
POOL_GROUPS = 4
POOL_GC = D_MODEL // POOL_GROUPS
POOL_HIST = max(POOL_WINDOWS) - 1
POOL_HALO = POOL_HIST + 1
ATTN_HEADS = 16
ATTN_HD = D_MODEL // ATTN_HEADS
N_PAST_CHUNKS = 8
ATTN_WINDOW = N_PAST_CHUNKS * CHUNK
REL_MIN = -(CHUNK - 1)
REL_MAX = 256
REL_SIZE = REL_MAX - REL_MIN + 1
HEAD_PAIRS = ATTN_HEADS // 2
PAIR_W = 2 * ATTN_HD
HG_HEADS = 8
HG_DK = D_MODEL // HG_HEADS
HG_DV = D_MODEL // HG_HEADS
HG_DIAG = 8
HG_SAFE_LOG2 = 100.0
LOG2E = math.log2(math.e)
D_FF = 4 * D_MODEL

V7X_VMEM_BYTES = 64 * 1024 * 1024
VMEM_LIMIT_BYTES = V7X_VMEM_BYTES - 8 * 1024 * 1024

NEG_INF = float("-inf")


def _params():
    return pltpu.CompilerParams(dimension_semantics=("arbitrary", "arbitrary"),
                                vmem_limit_bytes=VMEM_LIMIT_BYTES)


def _whole(shape):
    zeros = (0,) * len(shape)
    return pl.BlockSpec(shape, lambda s, t: zeros)


def _layer(shape, i, single_buffer=False):
    zeros = (0,) * len(shape)
    kw = dict(pipeline_mode=pl.Buffered(1)) if single_buffer else {}
    return pl.BlockSpec((None,) + tuple(shape), lambda s, t: (i,) + zeros, **kw)


def _rms(x, g):
    ms = jnp.mean(x * x, axis=-1, keepdims=True)
    return x * lax.rsqrt(ms + EPS) * g


def _rms_split(x, g):
    r = lax.rsqrt(jnp.mean(x * x, axis=-1, keepdims=True) + EPS)
    return (x * g).astype(BF16), r


def _dot(a, b):
    return jnp.dot(a, b, preferred_element_type=F32)


def _dot_tn(a, b):
    return lax.dot_general(a, b, (((0,), (0,)), ((), ())), preferred_element_type=F32)


def _split3(x):
    hi = x.astype(BF16)
    r1 = x - hi.astype(F32)
    mid = r1.astype(BF16)
    lo = (r1 - mid.astype(F32)).astype(BF16)
    return hi, mid, lo


def _mlp_apply(x, g, w1_ref, w2_ref, gf, final_norm):
    xg, r = _rms_split(x, g)
    h = _dot(xg, w1_ref[...])
    h = jnp.square(jnp.maximum(h, 0.0)).astype(BF16)
    y = x + (r * r) * _dot(h, w2_ref[...])
    if final_norm:
        y = _rms(y, gf)
    return y


def _mlp_kernel(x_ref, g_ref, w1_ref, w2_ref, gf_ref, o_ref, *, final_norm):
    o_ref[0] = _mlp_apply(x_ref[0], g_ref[...], w1_ref, w2_ref, gf_ref[...], final_norm)


def _mlp(x, g, w1, w2, gf, *, layer, tile, final_norm):
    S, L, D = x.shape
    assert L % tile == 0
    tok = pl.BlockSpec((1, tile, D), lambda s, t: (s, t, 0))
    return pl.pallas_call(
        functools.partial(_mlp_kernel, final_norm=final_norm),
        grid=(S, L // tile),
        in_specs=[tok, _whole((1, D)), _layer((D, D_FF), layer), _layer((D_FF, D), layer), _whole((1, D))],
        out_specs=tok,
        out_shape=jax.ShapeDtypeStruct((S, L, D), F32),
        compiler_params=_params(),
        name="mlp",
    )(x, g, w1, w2, gf)


def _pool_group(u, pos, ext_ref, w_ref, gi, tile):
    w = POOL_WINDOWS[gi]
    c0, c1 = gi * POOL_GC, (gi + 1) * POOL_GC
    win = u[:, c0:c1]
    for j in range(1, w):
        win = win + ext_ref[POOL_HALO - j:POOL_HALO - j + tile, c0:c1]
    cnt = jnp.minimum(pos + 1, w).astype(F32)
    diff = win / cnt - u[:, c0:c1]
    return _dot(diff.astype(BF16), w_ref[gi])


def _pool_apply(x, pos0, ext_ref, g, w_ref, sc, tile):
    u = _rms(x, g)
    ext_ref[POOL_HALO:POOL_HALO + tile, :] = u
    pos = pos0 + lax.broadcasted_iota(jnp.int32, (tile, 1), 0)
    ys = [_pool_group(u, pos, ext_ref, w_ref, gi, tile) for gi in range(POOL_GROUPS)]
    y = jnp.concatenate(ys, axis=1) * sc
    return x + y, ext_ref[tile:tile + POOL_HALO, :]


def _pool_kernel(x_ref, hist_ref, g_ref, w_ref, sc_ref, o_ref, cache_ref, ext_ref, *, tile, p0):
    t = pl.program_id(1)

    @pl.when(t == 0)
    def _():
        ext_ref[0:POOL_HALO, :] = hist_ref[0]

    out, tail = _pool_apply(x_ref[0], p0 + t * tile, ext_ref, g_ref[...], w_ref, sc_ref[...], tile)
    o_ref[0] = out
    cache_ref[0] = tail
    ext_ref[0:POOL_HALO, :] = tail


def _pool_mlp_kernel(x0_ref, xn_ref, hist_ref, gm_ref, pw_ref, sc_ref, gffn_ref, w1_ref, w2_ref, gf_ref,
                     o_ref, cache_ref, ext_ref, x1_ref, *, tile, p0, final_norm):
    t = pl.program_id(1)
    nt = pl.num_programs(1)

    @pl.when(t == 0)
    def _():
        ext_ref[0:POOL_HALO, :] = hist_ref[0]
        out, tail = _pool_apply(x0_ref[0], p0, ext_ref, gm_ref[...], pw_ref, sc_ref[...], tile)
        x1_ref[0] = out
        ext_ref[0:POOL_HALO, :] = tail

    x1 = x1_ref[lax.rem(t, 2)]
    v, r = _rms_split(x1, gffn_ref[...])
    halo = ext_ref[0:POOL_HALO, :]
    xn = xn_ref[0]
    u = _rms(xn, gm_ref[...])
    ext_ref[POOL_HALO:POOL_HALO + tile, :] = u
    pos = p0 + (t + 1) * tile + lax.broadcasted_iota(jnp.int32, (tile, 1), 0)
    fc = D_FF // POOL_GROUPS
    acc = None
    ys = []
    for c in range(POOL_GROUPS):
        h = _dot(v, w1_ref[:, c * fc:(c + 1) * fc])
        h = jnp.square(jnp.maximum(h, 0.0)).astype(BF16)
        part = _dot(h, w2_ref[c * fc:(c + 1) * fc, :])
        acc = part if acc is None else acc + part
        ys.append(_pool_group(u, pos, ext_ref, pw_ref, c, tile))
    acc = x1 + (r * r) * acc
    if final_norm:
        acc = _rms(acc, gf_ref[...])
    o_ref[0] = acc

    x1_ref[lax.rem(t + 1, 2)] = xn + jnp.concatenate(ys, axis=1) * sc_ref[...]
    halo = jnp.where(t + 1 < nt, ext_ref[tile:tile + POOL_HALO, :], halo)
    ext_ref[0:POOL_HALO, :] = halo
    cache_ref[0] = halo


def _pool_mlp(x, hist, gm, pw, sc, gffn, w1, w2, gf, *, pool_layer, layer, tile, p0, final_norm):
    S, L, D = x.shape
    nt = L // tile
    assert L % tile == 0 and tile >= POOL_HALO
    first = pl.BlockSpec((1, tile, D), lambda s, t: (s, 0, 0))
    nxt = pl.BlockSpec((1, tile, D), lambda s, t: (s, jnp.minimum(t + 1, nt - 1), 0))
    tok = pl.BlockSpec((1, tile, D), lambda s, t: (s, t, 0))
    seq = pl.BlockSpec((1, POOL_HALO, D), lambda s, t: (s, 0, 0))
    out, cache = pl.pallas_call(
        functools.partial(_pool_mlp_kernel, tile=tile, p0=p0, final_norm=final_norm),
        grid=(S, nt),
        in_specs=[first, nxt, seq, _whole((1, D)), _layer((POOL_GROUPS, POOL_GC, POOL_GC), pool_layer),
                  _whole((1, D)), _whole((1, D)),
                  _layer((D, D_FF), layer, single_buffer=True), _layer((D_FF, D), layer, single_buffer=True),
                  _whole((1, D))],
        out_specs=[tok, seq],
        out_shape=[jax.ShapeDtypeStruct((S, L, D), F32), jax.ShapeDtypeStruct((S, POOL_HALO, D), F32)],
        scratch_shapes=[pltpu.VMEM((POOL_HALO + tile, D), F32), pltpu.VMEM((2, tile, D), F32)],
        compiler_params=_params(),
        name="pool_mlp",
    )(x, x, hist, gm, pw, sc, gffn, w1, w2, gf)
    return out, cache[:, 1:, :]


def _pool(x, hist, g, w, sc, *, pool_layer, tile, p0):
    S, L, D = x.shape
    assert L % tile == 0 and tile >= POOL_HALO
    tok = pl.BlockSpec((1, tile, D), lambda s, t: (s, t, 0))
    seq = pl.BlockSpec((1, POOL_HALO, D), lambda s, t: (s, 0, 0))
    out, cache = pl.pallas_call(
        functools.partial(_pool_kernel, tile=tile, p0=p0),
        grid=(S, L // tile),
        in_specs=[tok, seq, _whole((1, D)), _layer((POOL_GROUPS, POOL_GC, POOL_GC), pool_layer), _whole((1, D))],
        out_specs=[tok, seq],
        out_shape=[jax.ShapeDtypeStruct((S, L, D), F32), jax.ShapeDtypeStruct((S, POOL_HALO, D), F32)],
        scratch_shapes=[pltpu.VMEM((POOL_HALO + tile, D), F32)],
        compiler_params=_params(),
        name="pool",
    )(x, hist, g, w, sc)
    return out, cache[:, 1:, :]


def _build_attn_bias(rb_ref, bias_ref, *, tile):
    nb = ATTN_WINDOW // tile + 1
    w2 = 2 * tile
    col = lax.broadcasted_iota(jnp.int32, (REL_SIZE, nb * w2), 1)
    row = lax.broadcasted_iota(jnp.int32, (REL_SIZE, nb * w2), 0)
    ip = col % w2
    i = jnp.where(ip < tile, ip, ip - w2)
    idx = jnp.clip((col // w2) * tile - i, REL_MIN, REL_MAX) - REL_MIN
    sel = jnp.where(row == idx, 1.0, 0.0).astype(BF16)
    hi, mid, lo = _split3(rb_ref[...])
    gen = _dot(hi, sel) + _dot(mid, sel) + _dot(lo, sel)

    r = lax.broadcasted_iota(jnp.int32, (tile, w2), 0)
    lane = lax.broadcasted_iota(jnp.int32, (tile, w2), 1)
    first = lane < tile
    for d in range(nb):
        dchunk = (d * tile) // CHUNK + r // CHUNK - (lane % tile) // CHUNK
        ok = (dchunk >= 0) & (dchunk <= N_PAST_CHUNKS)
        for p in range(HEAD_PAIRS):
            ga = jnp.broadcast_to(gen[2 * p:2 * p + 1, d * w2:(d + 1) * w2], (tile, w2))
            gb = jnp.broadcast_to(gen[2 * p + 1:2 * p + 2, d * w2:(d + 1) * w2], (tile, w2))
            ta = pltpu.roll(ga, 0, 1, stride=1, stride_axis=0)
            tb = pltpu.roll(gb, tile, 1, stride=1, stride_axis=0)
            bias_ref[p, d] = jnp.where(ok, jnp.where(first, ta, tb), NEG_INF)


def _attn_kernel(x_ref, hk_ref, hv_ref, g_ref, wqkv_ref, wo_ref, rb_ref,
                 o_ref, ko_ref, vo_ref, kt_ring, v_ring, bias_ref, *, tile, nsub, mask_start):
    D = D_MODEL
    nb = ATTN_WINDOW // tile + 1
    ring = ATTN_WINDOW // tile + nsub
    t = pl.program_id(1)

    row_lo = lax.broadcasted_iota(jnp.int32, (PAIR_W, tile), 0) < ATTN_HD
    lane_lo = lax.broadcasted_iota(jnp.int32, (tile, PAIR_W), 1) < ATTN_HD
    first = lax.broadcasted_iota(jnp.int32, (tile, 2 * tile), 1) < tile

    @pl.when((pl.program_id(0) == 0) & (t == 0))
    def _():
        _build_attn_bias(rb_ref, bias_ref, tile=tile)

    def store_block(slot, k, v):
        kt = k.T.astype(BF16)
        vb = v.astype(BF16)
        zk = jnp.zeros((PAIR_W, tile), BF16)
        zv = jnp.zeros((tile, PAIR_W), BF16)
        for p in range(HEAD_PAIRS):
            ktp = kt[p * PAIR_W:(p + 1) * PAIR_W, :]
            kt_ring[slot, p] = jnp.concatenate(
                [jnp.where(row_lo, ktp, zk), jnp.where(row_lo, zk, ktp)], axis=1)
            vp = vb[:, p * PAIR_W:(p + 1) * PAIR_W]
            v_ring[slot, p] = jnp.concatenate(
                [jnp.where(lane_lo, vp, zv), jnp.where(lane_lo, zv, vp)], axis=0)

    @pl.when(t == 0)
    def _():
        for i in range(nb - 1):
            store_block(i + nsub, hk_ref[0, i * tile:(i + 1) * tile, :], hv_ref[0, i * tile:(i + 1) * tile, :])

    x = x_ref[0]
    xg, r = _rms_split(x, g_ref[...])
    qkv = _dot(xg, wqkv_ref[...])
    q = (qkv[:, :D] * (r * (ATTN_HD ** -0.5))).astype(BF16)
    k = qkv[:, D:2 * D] * r
    v = qkv[:, 2 * D:] * r
    ko_ref[0] = k
    vo_ref[0] = v
    for j in range(nsub):
        store_block(lax.rem(t * nsub + j, ring), k[j * tile:(j + 1) * tile, :], v[j * tile:(j + 1) * tile, :])

    def slot_of(j, d):
        return lax.rem(t * nsub + j + ring - d, ring)

    def score_block(j, p, d):
        s = _dot(q[j * tile:(j + 1) * tile, p * PAIR_W:(p + 1) * PAIR_W], kt_ring[slot_of(j, d), p]) + bias_ref[p, d]
        if mask_start and d > j:
            s = s + jnp.where(t * nsub + j >= d, 0.0, NEG_INF)
        return s

    def row_max(scores):
        mx = functools.reduce(jnp.maximum, scores)
        m_a = jnp.max(mx[:, :tile], axis=-1, keepdims=True)
        m_b = jnp.max(mx[:, tile:], axis=-1, keepdims=True)
        return jnp.where(first, m_a, m_b)

    units = [(j, p) for j in range(nsub) for p in range(HEAD_PAIRS)]
    outs = [[] for _ in range(nsub)]
    nxt = [score_block(0, 0, d) for d in range(nb)]
    for n, (j, p) in enumerate(units):
        scores, m = nxt, row_max(nxt)
        if n + 1 < len(units):
            nxt = [score_block(units[n + 1][0], units[n + 1][1], d) for d in range(nb)]
        lsum = jnp.zeros((tile, 2 * tile), F32)
        acc = jnp.zeros((tile, PAIR_W), F32)
        for d in range(nb):
            pr = jnp.exp(scores[d] - m)
            lsum = lsum + pr
            acc = acc + _dot(pr.astype(BF16), v_ring[slot_of(j, d), p])
        l_a = jnp.sum(lsum[:, :tile], axis=-1, keepdims=True)
        l_b = jnp.sum(lsum[:, tile:], axis=-1, keepdims=True)
        inv = jnp.where(lane_lo, 1.0 / l_a, 1.0 / l_b)
        outs[j].append((acc * inv).astype(BF16))
    o = jnp.concatenate([jnp.concatenate(row, axis=1) for row in outs], axis=0)
    o_ref[0] = x + _dot(o, wo_ref[...])


def _attn(x, hist_k, hist_v, g, wqkv, wo, rel_bias, *, tile, nsub, keep, mask_start):
    S, L, D = x.shape
    step = tile * nsub
    nb = ATTN_WINDOW // tile + 1
    nt = L // step
    nkeep = keep // step
    assert L % step == 0 and tile % CHUNK == 0 and ATTN_WINDOW % tile == 0 and keep % step == 0
    tok = pl.BlockSpec((1, step, D), lambda s, t: (s, t, 0))
    hist = pl.BlockSpec((1, ATTN_WINDOW, D), lambda s, t: (s, 0, 0))
    kv_out = pl.BlockSpec((1, step, D), lambda s, t: (s, jnp.maximum(t - (nt - nkeep), 0), 0))
    return pl.pallas_call(
        functools.partial(_attn_kernel, tile=tile, nsub=nsub, mask_start=mask_start),
        grid=(S, nt),
        in_specs=[tok, hist, hist, _whole((1, D)), _whole((D, 3 * D)), _whole((D, D)),
                  _whole((ATTN_HEADS, REL_SIZE))],
        out_specs=[tok, kv_out, kv_out],
        out_shape=[jax.ShapeDtypeStruct((S, L, D), F32),
                   jax.ShapeDtypeStruct((S, keep, D), F32),
                   jax.ShapeDtypeStruct((S, keep, D), F32)],
        scratch_shapes=[pltpu.VMEM((nb - 1 + nsub, HEAD_PAIRS, PAIR_W, 2 * tile), BF16),
                        pltpu.VMEM((nb - 1 + nsub, HEAD_PAIRS, 2 * tile, PAIR_W), BF16),
                        pltpu.VMEM((HEAD_PAIRS, nb, tile, 2 * tile), F32)],
        compiler_params=_params(),
        name="attn",
    )(x, hist_k, hist_v, g, wqkv, wo, rel_bias)


def _tile_row(x, n, s):
    C, W = x.shape
    xr = x.reshape(C // n, n, W)
    return jnp.broadcast_to(xr[:, s:s + 1, :], (C // n, n, W)).reshape(C, W)


def _hgrn_intra_fast(q, kk, b2, vb, tril):
    C = q.shape[0]
    bmid = b2[C // 2 - 1:C // 2, :]
    qt = (q * jnp.exp2(b2 - bmid)).astype(BF16)
    kt = (kk * jnp.exp2(bmid - b2)).T.astype(BF16)
    amat = jnp.where(tril, _dot(qt, kt), 0.0)
    return _dot(amat.astype(BF16), vb)


def _hgrn_intra_safe(q, kk, b2, vv, vb, levels, rowid, ones):
    C = q.shape[0]
    amat = jnp.zeros((C, C), F32)
    for hl, upper, same in levels:
        bref = _tile_row(b2, 2 * hl, hl - 1)
        qt = q * jnp.exp2(jnp.where(upper, b2 - bref, NEG_INF))
        kt = kk * jnp.exp2(jnp.where(upper, NEG_INF, bref - b2))
        amat = amat + jnp.where(same, _dot(qt.astype(BF16), kt.T.astype(BF16)), 0.0)
    o = _dot(amat.astype(BF16), vb)
    sub = rowid % HG_DIAG
    for s in range(HG_DIAG):
        dec = jnp.exp2(jnp.where(sub >= s, b2 - _tile_row(b2, HG_DIAG, s), NEG_INF))
        ps = q * _tile_row(kk, HG_DIAG, s) * dec
        o = o + _dot(ps.astype(BF16), ones) * _tile_row(vv, HG_DIAG, s)
    return o


def _hgrn_kernel(x_ref, s0_ref, g_ref, win_ref, wo_ref, gn_ref, lbp_ref,
                 o_ref, so_ref, st_ref, pj_ref, b2_ref, og_ref, *, chunk, nsub, layer):
    D = D_MODEL
    C = chunk
    t = pl.program_id(1)
    nt = pl.num_programs(1)

    @pl.when(t == 0)
    def _():
        for h in range(HG_HEADS):
            st_ref[h] = s0_ref[0, h].T

    x = x_ref[0]
    xg, rinv = _rms_split(x, g_ref[...])
    pj_ref[...] = _dot(xg, win_ref[...]) * rinv

    lbp = lbp_ref[...]
    e = jnp.exp(lbp - jnp.max(lbp, axis=0, keepdims=True))
    sm = e / jnp.sum(e, axis=0, keepdims=True)
    lb = jnp.zeros((1, D), F32)
    for i in range(1, layer + 1):
        lb = lb + sm[i:i + 1, :]
    log_lb = jnp.log(lb)
    log_1mlb = jnp.log1p(-lb)

    zf = pj_ref[:, D:2 * D]
    a = jnp.exp(-jnp.abs(zf))
    r = 1.0 / (1.0 + a)
    log_sig = jnp.minimum(zf, 0.0) - jnp.log(1.0 + a)
    bb = log_1mlb + log_sig
    logf = jnp.maximum(log_lb, bb) + jnp.log(1.0 + jnp.exp(-jnp.abs(log_lb - bb)))
    pj_ref[:, D:2 * D] = (1.0 - lb) * jnp.where(zf >= 0.0, a * r, r)
    zq = pj_ref[:, :D]
    pj_ref[:, :D] = zq * (1.0 / (1.0 + jnp.exp(-zq)))

    ri = lax.broadcasted_iota(jnp.int32, (C, C), 0)
    ci = lax.broadcasted_iota(jnp.int32, (C, C), 1)
    tril = ri >= ci
    tri = tril.astype(BF16)
    ends = []
    safe = None
    for c in range(nsub):
        hi, mid, lo = _split3(logf[c * C:(c + 1) * C, :])
        b2_c = (_dot(tri, hi) + _dot(tri, mid) + _dot(tri, lo)) * LOG2E
        b2_ref[c * C:(c + 1) * C, :] = b2_c
        b2_mid = b2_c[C // 2 - 1:C // 2, :]
        b2_end = b2_c[C - 1:C, :]
        ends.append(b2_end)
        ok = (jnp.min(b2_mid) >= -HG_SAFE_LOG2) & (jnp.min(b2_end - b2_mid) >= -HG_SAFE_LOG2)
        safe = ok if safe is None else safe & ok

    def finish(intra):
        for c in range(nsub):
            rows = slice(c * C, (c + 1) * C)
            for h in range(HG_HEADS):
                cols = slice(h * HG_DK, (h + 1) * HG_DK)
                q, kk, b2 = pj_ref[rows, cols], pj_ref[rows, D + h * HG_DK:D + (h + 1) * HG_DK], b2_ref[rows, cols]
                vv = pj_ref[rows, 2 * D + h * HG_DV:2 * D + (h + 1) * HG_DV]
                zg = pj_ref[rows, 3 * D + h * HG_DV:3 * D + (h + 1) * HG_DV]
                vb = vv.astype(BF16)
                st = st_ref[h]
                end = ends[c][:, cols]
                o = _dot((q * jnp.exp2(b2)).astype(BF16), st.T.astype(BF16))
                kdec = (kk * jnp.exp2(end - b2)).astype(BF16)
                st_ref[h] = st * jnp.exp2(end) + _dot_tn(vb, kdec)
                o = o + intra(q, kk, b2, vv, vb)
                on = o * lax.rsqrt(jnp.mean(o * o, axis=-1, keepdims=True) + EPS) * gn_ref[...]
                og_ref[rows, h * HG_DV:(h + 1) * HG_DV] = (on * (zg * (1.0 / (1.0 + jnp.exp(-zg))))).astype(BF16)
        o_ref[0] = x + _dot(og_ref[...], wo_ref[...])

    @pl.when(safe)
    def _():
        finish(lambda q, kk, b2, vv, vb: _hgrn_intra_fast(q, kk, b2, vb, tril))

    @pl.when(jnp.logical_not(safe))
    def _():
        rowid = lax.broadcasted_iota(jnp.int32, (C, 1), 0)
        ones = jnp.ones((HG_DK, HG_DV), BF16)
        levels = []
        hl = C // 2
        while hl >= HG_DIAG:
            levels.append((hl, (rowid % (2 * hl)) >= hl, (ri // (2 * hl)) == (ci // (2 * hl))))
            hl //= 2
        finish(lambda q, kk, b2, vv, vb: _hgrn_intra_safe(q, kk, b2, vv, vb, levels, rowid, ones))

    @pl.when(t == nt - 1)
    def _():
        for h in range(HG_HEADS):
            so_ref[0, h] = st_ref[h].T


def _hgrn(x, s0, g, w_in, w_o, gn, lbp, *, chunk, nsub, layer):
    S, L, D = x.shape
    tile = chunk * nsub
    assert L % tile == 0 and chunk % (2 * HG_DIAG) == 0
    tok = pl.BlockSpec((1, tile, D), lambda s, t: (s, t, 0))
    state = pl.BlockSpec((1, HG_HEADS, HG_DK, HG_DV), lambda s, t: (s, 0, 0, 0))
    return pl.pallas_call(
        functools.partial(_hgrn_kernel, chunk=chunk, nsub=nsub, layer=layer),
        grid=(S, L // tile),
        in_specs=[tok, state, _whole((1, D)), _whole((D, 4 * D)), _whole((D, D)), _whole((1, HG_DV)),
                  _whole((DEPTH, D))],
        out_specs=[tok, state],
        out_shape=[jax.ShapeDtypeStruct((S, L, D), F32),
                   jax.ShapeDtypeStruct((S, HG_HEADS, HG_DK, HG_DV), F32)],
        scratch_shapes=[pltpu.VMEM((HG_HEADS, HG_DV, HG_DK), F32),
                        pltpu.VMEM((tile, 4 * D), F32),
                        pltpu.VMEM((tile, D), F32),
                        pltpu.VMEM((tile, D), BF16)],
        compiler_params=_params(),
        name="hgrn",
    )(x, s0, g, w_in, w_o, gn, lbp)


MLP_TILE = (512, 512)
POOL_TILE = (512, 64)
ATTN_TILE = (128, 64)
ATTN_NSUB = (2, 1)
HGRN_CHUNK = (128, 64)
HGRN_NSUB = (4, 1)


def kernel(x_prompt, x_sample, cache_pool, cache_attn_k, cache_attn_v, state_hgrn, norm_mix, norm_ffn,
           norm_final, pool_w, pool_scale, attn_wqkv, attn_wo, attn_rel_bias, hgrn_w_in, hgrn_w_o,
           hgrn_gnorm, hgrn_lower_bounds, mlp_w1, mlp_w2):
    B, L, D = x_prompt.shape
    BS, LS, _ = x_sample.shape
    xp = x_prompt
    xs = x_sample
    gf = norm_final.reshape(1, D)
    w1_all = mlp_w1.astype(BF16)
    w2_all = mlp_w2.astype(BF16)
    pw_all = pool_w.astype(BF16)

    pool_p, pool_s, k_p, k_s, v_p, v_s, h_p, h_s = [], [], [], [], [], [], [], []
    for i in range(DEPTH):
        kind, j = i % N_MIXERS, i // N_MIXERS
        g = norm_mix[i].reshape(1, D)
        gm = norm_ffn[i].reshape(1, D)
        last = i == DEPTH - 1
        if kind == 0:
            sc = pool_scale[j].reshape(1, D)
            hist_p = jnp.zeros((B, POOL_HALO, D), F32)
            hist_s = jnp.pad(cache_pool[j], ((0, 0), (1, 0), (0, 0)))
            xp, cp = _pool_mlp(xp, hist_p, g, pw_all, sc, gm, w1_all, w2_all, gf, pool_layer=j, layer=i,
                               tile=MLP_TILE[0], p0=0, final_norm=last)
            xs, cs = _pool(xs, hist_s, g, pw_all, sc, pool_layer=j, tile=POOL_TILE[1], p0=PAST_LEN)
            pool_p.append(cp)
            pool_s.append(cs)
        elif kind == 1:
            wqkv = attn_wqkv[j].astype(BF16)
            wo = attn_wo[j].astype(BF16)
            keep_p = min(ATTN_WINDOW, L)
            zeros = jnp.zeros((B, ATTN_WINDOW, D), F32)
            xp, kp, vp = _attn(xp, zeros, zeros, g, wqkv, wo, attn_rel_bias[j],
                               tile=ATTN_TILE[0], nsub=ATTN_NSUB[0], keep=keep_p, mask_start=True)
            ck = cache_attn_k[j].reshape(BS, ATTN_WINDOW, D)
            cv = cache_attn_v[j].reshape(BS, ATTN_WINDOW, D)
            xs, ks, vs = _attn(xs, ck, cv, g, wqkv, wo, attn_rel_bias[j],
                               tile=ATTN_TILE[1], nsub=ATTN_NSUB[1], keep=LS, mask_start=False)
            k_p.append(kp.reshape(B, keep_p, ATTN_HEADS, ATTN_HD))
            v_p.append(vp.reshape(B, keep_p, ATTN_HEADS, ATTN_HD))
            k_s.append(ks.reshape(BS, LS, ATTN_HEADS, ATTN_HD))
            v_s.append(vs.reshape(BS, LS, ATTN_HEADS, ATTN_HD))
        else:
            w_in = hgrn_w_in[j].astype(BF16)
            w_o = hgrn_w_o[j].astype(BF16)
            gn = hgrn_gnorm[j].reshape(1, HG_DV)
            s0 = jnp.zeros((B, HG_HEADS, HG_DK, HG_DV), F32)
            xp, sp = _hgrn(xp, s0, g, w_in, w_o, gn, hgrn_lower_bounds, chunk=HGRN_CHUNK[0], nsub=HGRN_NSUB[0],
                           layer=i)
            xs, ss = _hgrn(xs, state_hgrn[j], g, w_in, w_o, gn, hgrn_lower_bounds, chunk=HGRN_CHUNK[1],
                           nsub=HGRN_NSUB[1], layer=i)
            h_p.append(sp)
            h_s.append(ss)
        if kind != 0:
            xp = _mlp(xp, gm, w1_all, w2_all, gf, layer=i, tile=MLP_TILE[0], final_norm=last)
        xs = _mlp(xs.reshape(1, BS * LS, D), gm, w1_all, w2_all, gf, layer=i, tile=MLP_TILE[1],
                  final_norm=last).reshape(BS, LS, D)

    return (xp, xs, jnp.stack(pool_p), jnp.stack(pool_s), jnp.stack(k_p), jnp.stack(k_s),
            jnp.stack(v_p), jnp.stack(v_s), jnp.stack(h_p), jnp.stack(h_s))
```

```python
import functools
import math

import jax
import jax.numpy as jnp
from jax import lax
from jax.experimental import pallas as pl
from jax.experimental.pallas import tpu as pltpu

F32 = jnp.float32
BF16 = jnp.bfloat16

D_MODEL = 1024
DEPTH = 4
CHUNK = 64
N_MIXERS = 3
PAST_LEN = 2048
EPS = 1e-6
POOL_WINDOWS = (2, 4, 8, 16)
POOL_GROUPS = 4
POOL_GC = D_MODEL // POOL_GROUPS
POOL_HIST = max(POOL_WINDOWS) - 1
POOL_HALO = POOL_HIST + 1
ATTN_HEADS = 16
ATTN_HD = D_MODEL // ATTN_HEADS
N_PAST_CHUNKS = 8
ATTN_WINDOW = N_PAST_CHUNKS * CHUNK
REL_MIN = -(CHUNK - 1)
REL_MAX = 256
REL_SIZE = REL_MAX - REL_MIN + 1
HEAD_PAIRS = ATTN_HEADS // 2
PAIR_W = 2 * ATTN_HD
HG_HEADS = 8
HG_DK = D_MODEL // HG_HEADS
HG_DV = D_MODEL // HG_HEADS
HG_DIAG = 8
HG_SAFE_LOG2 = 100.0
LOG2E = math.log2(math.e)
D_FF = 4 * D_MODEL

V7X_VMEM_BYTES = 64 * 1024 * 1024
VMEM_LIMIT_BYTES = V7X_VMEM_BYTES - 8 * 1024 * 1024

NEG_INF = float("-inf")


def _params():
    return pltpu.CompilerParams(dimension_semantics=("arbitrary", "arbitrary"),
                                vmem_limit_bytes=VMEM_LIMIT_BYTES)


def _whole(shape):
    zeros = (0,) * len(shape)
    return pl.BlockSpec(shape, lambda s, t: zeros)


def _layer(shape, i, single_buffer=False):
    zeros = (0,) * len(shape)
    kw = dict(pipeline_mode=pl.Buffered(1)) if single_buffer else {}
    return pl.BlockSpec((None,) + tuple(shape), lambda s, t: (i,) + zeros, **kw)


def _rms(x, g):
    ms = jnp.mean(x * x, axis=-1, keepdims=True)
    return x * lax.rsqrt(ms + EPS) * g


def _rms_split(x, g):
    r = lax.rsqrt(jnp.mean(x * x, axis=-1, keepdims=True) + EPS)
    return (x * g).astype(BF16), r


def _dot(a, b):
    return jnp.dot(a, b, preferred_element_type=F32)


def _dot_tn(a, b):
    return lax.dot_general(a, b, (((0,), (0,)), ((), ())), preferred_element_type=F32)


def _split3(x):
    hi = x.astype(BF16)
    r1 = x - hi.astype(F32)
    mid = r1.astype(BF16)
    lo = (r1 - mid.astype(F32)).astype(BF16)
    return hi, mid, lo


def _mlp_apply(x, g, w1_ref, w2_ref, gf, final_norm):
    xg, r = _rms_split(x, g)
    h = _dot(xg, w1_ref[...])
    h = jnp.square(jnp.maximum(h, 0.0)).astype(BF16)
    y = x + (r * r) * _dot(h, w2_ref[...])
    if final_norm:
        y = _rms(y, gf)
    return y


def _mlp_kernel(xp_ref, xs_ref, g_ref, w1_ref, w2_ref, gf_ref, op_ref, os_ref, *, final_norm):
    t = pl.program_id(1)
    last = pl.num_programs(1) - 1

    @pl.when(t < last)
    def _():
        op_ref[0] = _mlp_apply(xp_ref[0], g_ref[...], w1_ref, w2_ref, gf_ref[...], final_norm)

    @pl.when(t == last)
    def _():
        os_ref[0] = _mlp_apply(xs_ref[0], g_ref[...], w1_ref, w2_ref, gf_ref[...], final_norm)


def _mlp(xp, xs, g, w1, w2, gf, *, layer, tile, final_norm):
    _, Lp, D = xp.shape
    _, Ls, _ = xs.shape
    ntp = Lp // tile
    assert Lp % tile == 0
    tok = pl.BlockSpec((1, tile, D), lambda s, t: (0, jnp.minimum(t, ntp - 1), 0))
    smp = pl.BlockSpec((1, Ls, D), lambda s, t: (0, 0, 0))
    return pl.pallas_call(
        functools.partial(_mlp_kernel, final_norm=final_norm),
        grid=(1, ntp + 1),
        in_specs=[tok, smp, _whole((1, D)), _layer((D, D_FF), layer), _layer((D_FF, D), layer), _whole((1, D))],
        out_specs=[tok, smp],
        out_shape=[jax.ShapeDtypeStruct((1, Lp, D), F32), jax.ShapeDtypeStruct((1, Ls, D), F32)],
        compiler_params=_params(),
        name="mlp",
    )(xp, xs, g, w1, w2, gf)


def _pool_group(u, pos, ext_ref, w_ref, gi, tile):
    w = POOL_WINDOWS[gi]
    c0, c1 = gi * POOL_GC, (gi + 1) * POOL_GC
    win = u[:, c0:c1]
    for j in range(1, w):
        win = win + ext_ref[POOL_HALO - j:POOL_HALO - j + tile, c0:c1]
    cnt = jnp.minimum(pos + 1, w).astype(F32)
    diff = win / cnt - u[:, c0:c1]
    return _dot(diff.astype(BF16), w_ref[gi])


def _pool_apply(x, pos0, ext_ref, g, w_ref, sc, tile):
    u = _rms(x, g)
    ext_ref[POOL_HALO:POOL_HALO + tile, :] = u
    pos = pos0 + lax.broadcasted_iota(jnp.int32, (tile, 1), 0)
    ys = [_pool_group(u, pos, ext_ref, w_ref, gi, tile) for gi in range(POOL_GROUPS)]
    y = jnp.concatenate(ys, axis=1) * sc
    return x + y, ext_ref[tile:tile + POOL_HALO, :]


def _pool_kernel(x_ref, hist_ref, g_ref, w_ref, sc_ref, o_ref, cache_ref, ext_ref, *, tile, p0):
    t = pl.program_id(1)

    @pl.when(t == 0)
    def _():
        ext_ref[0:POOL_HALO, :] = hist_ref[0]

    out, tail = _pool_apply(x_ref[0], p0 + t * tile, ext_ref, g_ref[...], w_ref, sc_ref[...], tile)
    o_ref[0] = out
    cache_ref[0] = tail
    ext_ref[0:POOL_HALO, :] = tail


def _pool_mlp_kernel(x0_ref, xn_ref, xs_ref, hist_ref, gm_ref, pw_ref, sc_ref, gffn_ref, w1_ref, w2_ref, gf_ref,
                     o_ref, cache_ref, os_ref, ext_ref, x1_ref, *, tile, p0, final_norm):
    t = pl.program_id(1)
    nt = pl.num_programs(1) - 1

    @pl.when(t == nt)
    def _():
        os_ref[0] = _mlp_apply(xs_ref[0], gffn_ref[...], w1_ref, w2_ref, gf_ref[...], final_norm)

    @pl.when(t == 0)
    def _():
        ext_ref[0:POOL_HALO, :] = hist_ref[0]
        out, tail = _pool_apply(x0_ref[0], p0, ext_ref, gm_ref[...], pw_ref, sc_ref[...], tile)
        x1_ref[0] = out
        ext_ref[0:POOL_HALO, :] = tail

    @pl.when(t < nt)
    def _():
        x1 = x1_ref[lax.rem(t, 2)]
        v, r = _rms_split(x1, gffn_ref[...])
        halo = ext_ref[0:POOL_HALO, :]
        xn = xn_ref[0]
        u = _rms(xn, gm_ref[...])
        ext_ref[POOL_HALO:POOL_HALO + tile, :] = u
        pos = p0 + (t + 1) * tile + lax.broadcasted_iota(jnp.int32, (tile, 1), 0)
        fc = D_FF // POOL_GROUPS
        acc = None
        ys = []
        for c in range(POOL_GROUPS):
            h = _dot(v, w1_ref[:, c * fc:(c + 1) * fc])
            h = jnp.square(jnp.maximum(h, 0.0)).astype(BF16)
            part = _dot(h, w2_ref[c * fc:(c + 1) * fc, :])
            acc = part if acc is None else acc + part
            ys.append(_pool_group(u, pos, ext_ref, pw_ref, c, tile))
        acc = x1 + (r * r) * acc
        if final_norm:
            acc = _rms(acc, gf_ref[...])
        o_ref[0] = acc

        x1_ref[lax.rem(t + 1, 2)] = xn + jnp.concatenate(ys, axis=1) * sc_ref[...]
        halo = jnp.where(t + 1 < nt, ext_ref[tile:tile + POOL_HALO, :], halo)
        ext_ref[0:POOL_HALO, :] = halo
        cache_ref[0] = halo


def _pool_mlp(x, xs, hist, gm, pw, sc, gffn, w1, w2, gf, *, pool_layer, layer, tile, p0, final_norm):
    S, L, D = x.shape
    _, Ls, _ = xs.shape
    nt = L // tile
    assert S == 1 and L % tile == 0 and tile >= POOL_HALO
    first = pl.BlockSpec((1, tile, D), lambda s, t: (s, 0, 0))
    nxt = pl.BlockSpec((1, tile, D), lambda s, t: (s, jnp.minimum(t + 1, nt - 1), 0))
    tok = pl.BlockSpec((1, tile, D), lambda s, t: (s, jnp.minimum(t, nt - 1), 0))
    smp = pl.BlockSpec((1, Ls, D), lambda s, t: (0, 0, 0))
    seq = pl.BlockSpec((1, POOL_HALO, D), lambda s, t: (s, 0, 0))
    out, cache, outs = pl.pallas_call(
        functools.partial(_pool_mlp_kernel, tile=tile, p0=p0, final_norm=final_norm),
        grid=(S, nt + 1),
        in_specs=[first, nxt, smp, seq, _whole((1, D)), _layer((POOL_GROUPS, POOL_GC, POOL_GC), pool_layer),
                  _whole((1, D)), _whole((1, D)),
                  _layer((D, D_FF), layer, single_buffer=True), _layer((D_FF, D), layer, single_buffer=True),
                  _whole((1, D))],
        out_specs=[tok, seq, smp],
        out_shape=[jax.ShapeDtypeStruct((S, L, D), F32), jax.ShapeDtypeStruct((S, POOL_HALO, D), F32),
                   jax.ShapeDtypeStruct((1, Ls, D), F32)],
        scratch_shapes=[pltpu.VMEM((POOL_HALO + tile, D), F32), pltpu.VMEM((2, tile, D), F32)],
        compiler_params=_params(),
        name="pool_mlp",
    )(x, x, xs, hist, gm, pw, sc, gffn, w1, w2, gf)
    return out, cache[:, 1:, :], outs


def _pool(x, hist, g, w, sc, *, pool_layer, tile, p0):
    S, L, D = x.shape
    assert L % tile == 0 and tile >= POOL_HALO
    tok = pl.BlockSpec((1, tile, D), lambda s, t: (s, t, 0))
    seq = pl.BlockSpec((1, POOL_HALO, D), lambda s, t: (s, 0, 0))
    out, cache = pl.pallas_call(
        functools.partial(_pool_kernel, tile=tile, p0=p0),
        grid=(S, L // tile),
        in_specs=[tok, seq, _whole((1, D)), _layer((POOL_GROUPS, POOL_GC, POOL_GC), pool_layer), _whole((1, D))],
        out_specs=[tok, seq],
        out_shape=[jax.ShapeDtypeStruct((S, L, D), F32), jax.ShapeDtypeStruct((S, POOL_HALO, D), F32)],
        scratch_shapes=[pltpu.VMEM((POOL_HALO + tile, D), F32)],
        compiler_params=_params(),
        name="pool",
    )(x, hist, g, w, sc)
    return out, cache[:, 1:, :]


def _build_attn_bias(rb_ref, bias_ref, *, tile):
    nb = ATTN_WINDOW // tile + 1
    w2 = 2 * tile
    col = lax.broadcasted_iota(jnp.int32, (REL_SIZE, nb * w2), 1)
    row = lax.broadcasted_iota(jnp.int32, (REL_SIZE, nb * w2), 0)
    ip = col % w2
    i = jnp.where(ip < tile, ip, ip - w2)
    idx = jnp.clip((col // w2) * tile - i, REL_MIN, REL_MAX) - REL_MIN
    sel = jnp.where(row == idx, 1.0, 0.0).astype(BF16)
    hi, mid, lo = _split3(rb_ref[...])
    gen = _dot(hi, sel) + _dot(mid, sel) + _dot(lo, sel)

    r = lax.broadcasted_iota(jnp.int32, (tile, w2), 0)
    lane = lax.broadcasted_iota(jnp.int32, (tile, w2), 1)
    first = lane < tile
    for d in range(nb):
        dchunk = (d * tile) // CHUNK + r // CHUNK - (lane % tile) // CHUNK
        ok = (dchunk >= 0) & (dchunk <= N_PAST_CHUNKS)
        for p in range(HEAD_PAIRS):
            ga = jnp.broadcast_to(gen[2 * p:2 * p + 1, d * w2:(d + 1) * w2], (tile, w2))
            gb = jnp.broadcast_to(gen[2 * p + 1:2 * p + 2, d * w2:(d + 1) * w2], (tile, w2))
            ta = pltpu.roll(ga, 0, 1, stride=1, stride_axis=0)
            tb = pltpu.roll(gb, tile, 1, stride=1, stride_axis=0)
            bias_ref[p, d] = jnp.where(ok, jnp.where(first, ta, tb), NEG_INF)


def _attn_kernel(x_ref, hk_ref, hv_ref, g_ref, wqkv_ref, wo_ref, rb_ref,
                 o_ref, ko_ref, vo_ref, kt_ring, v_ring, bias_ref, *, tile, nsub, mask_start):
    D = D_MODEL
    nb = ATTN_WINDOW // tile + 1
    ring = ATTN_WINDOW // tile + nsub
    t = pl.program_id(1)

    row_lo = lax.broadcasted_iota(jnp.int32, (PAIR_W, tile), 0) < ATTN_HD
    lane_lo = lax.broadcasted_iota(jnp.int32, (tile, PAIR_W), 1) < ATTN_HD
    first = lax.broadcasted_iota(jnp.int32, (tile, 2 * tile), 1) < tile

    @pl.when((pl.program_id(0) == 0) & (t == 0))
    def _():
        _build_attn_bias(rb_ref, bias_ref, tile=tile)

    def store_block(slot, k, v):
        kt = k.T.astype(BF16)
        vb = v.astype(BF16)
        zk = jnp.zeros((PAIR_W, tile), BF16)
        zv = jnp.zeros((tile, PAIR_W), BF16)
        for p in range(HEAD_PAIRS):
            ktp = kt[p * PAIR_W:(p + 1) * PAIR_W, :]
            kt_ring[slot, p] = jnp.concatenate(
                [jnp.where(row_lo, ktp, zk), jnp.where(row_lo, zk, ktp)], axis=1)
            vp = vb[:, p * PAIR_W:(p + 1) * PAIR_W]
            v_ring[slot, p] = jnp.concatenate(
                [jnp.where(lane_lo, vp, zv), jnp.where(lane_lo, zv, vp)], axis=0)

    @pl.when(t == 0)
    def _():
        for i in range(nb - 1):
            store_block(i + nsub, hk_ref[0, i * tile:(i + 1) * tile, :], hv_ref[0, i * tile:(i + 1) * tile, :])

    x = x_ref[0]
    xg, r = _rms_split(x, g_ref[...])
    qkv = _dot(xg, wqkv_ref[...])
    q = (qkv[:, :D] * (r * (ATTN_HD ** -0.5))).astype(BF16)
    k = qkv[:, D:2 * D] * r
    v = qkv[:, 2 * D:] * r
    ko_ref[0] = k
    vo_ref[0] = v
    for j in range(nsub):
        store_block(lax.rem(t * nsub + j, ring), k[j * tile:(j + 1) * tile, :], v[j * tile:(j + 1) * tile, :])

    def slot_of(j, d):
        return lax.rem(t * nsub + j + ring - d, ring)

    def score_block(j, p, d):
        s = _dot(q[j * tile:(j + 1) * tile, p * PAIR_W:(p + 1) * PAIR_W], kt_ring[slot_of(j, d), p]) + bias_ref[p, d]
        if mask_start and d > j:
            s = s + jnp.where(t * nsub + j >= d, 0.0, NEG_INF)
        return s

    def row_max(scores):
        mx = functools.reduce(jnp.maximum, scores)
        m_a = jnp.max(mx[:, :tile], axis=-1, keepdims=True)
        m_b = jnp.max(mx[:, tile:], axis=-1, keepdims=True)
        return jnp.where(first, m_a, m_b)

    units = [(j, p) for j in range(nsub) for p in range(HEAD_PAIRS)]
    outs = [[] for _ in range(nsub)]
    nxt = [score_block(0, 0, d) for d in range(nb)]
    for n, (j, p) in enumerate(units):
        scores, m = nxt, row_max(nxt)
        if n + 1 < len(units):
            nxt = [score_block(units[n + 1][0], units[n + 1][1], d) for d in range(nb)]
        lsum = jnp.zeros((tile, 2 * tile), F32)
        acc = jnp.zeros((tile, PAIR_W), F32)
        for d in range(nb):
            pr = jnp.exp(scores[d] - m)
            lsum = lsum + pr
            acc = acc + _dot(pr.astype(BF16), v_ring[slot_of(j, d), p])
        l_a = jnp.sum(lsum[:, :tile], axis=-1, keepdims=True)
        l_b = jnp.sum(lsum[:, tile:], axis=-1, keepdims=True)
        inv = jnp.where(lane_lo, 1.0 / l_a, 1.0 / l_b)
        outs[j].append((acc * inv).astype(BF16))
    o = jnp.concatenate([jnp.concatenate(row, axis=1) for row in outs], axis=0)
    o_ref[0] = x + _dot(o, wo_ref[...])


def _attn(x, hist_k, hist_v, g, wqkv, wo, rel_bias, *, tile, nsub, keep, mask_start):
    S, L, D = x.shape
    step = tile * nsub
    nb = ATTN_WINDOW // tile + 1
    nt = L // step
    nkeep = keep // step
    assert L % step == 0 and tile % CHUNK == 0 and ATTN_WINDOW % tile == 0 and keep % step == 0
    tok = pl.BlockSpec((1, step, D), lambda s, t: (s, t, 0))
    hist = pl.BlockSpec((1, ATTN_WINDOW, D), lambda s, t: (s, 0, 0))
    kv_out = pl.BlockSpec((1, step, D), lambda s, t: (s, jnp.maximum(t - (nt - nkeep), 0), 0))
    return pl.pallas_call(
        functools.partial(_attn_kernel, tile=tile, nsub=nsub, mask_start=mask_start),
        grid=(S, nt),
        in_specs=[tok, hist, hist, _whole((1, D)), _whole((D, 3 * D)), _whole((D, D)),
                  _whole((ATTN_HEADS, REL_SIZE))],
        out_specs=[tok, kv_out, kv_out],
        out_shape=[jax.ShapeDtypeStruct((S, L, D), F32),
                   jax.ShapeDtypeStruct((S, keep, D), F32),
                   jax.ShapeDtypeStruct((S, keep, D), F32)],
        scratch_shapes=[pltpu.VMEM((nb - 1 + nsub, HEAD_PAIRS, PAIR_W, 2 * tile), BF16),
                        pltpu.VMEM((nb - 1 + nsub, HEAD_PAIRS, 2 * tile, PAIR_W), BF16),
                        pltpu.VMEM((HEAD_PAIRS, nb, tile, 2 * tile), F32)],
        compiler_params=_params(),
        name="attn",
    )(x, hist_k, hist_v, g, wqkv, wo, rel_bias)


def _tile_row(x, n, s):
    C, W = x.shape
    xr = x.reshape(C // n, n, W)
    return jnp.broadcast_to(xr[:, s:s + 1, :], (C // n, n, W)).reshape(C, W)


def _hgrn_intra_fast(q, kk, b2, vb, tril):
    C = q.shape[0]
    bmid = b2[C // 2 - 1:C // 2, :]
    qt = (q * jnp.exp2(b2 - bmid)).astype(BF16)
    kt = (kk * jnp.exp2(bmid - b2)).T.astype(BF16)
    amat = jnp.where(tril, _dot(qt, kt), 0.0)
    return _dot(amat.astype(BF16), vb)


def _hgrn_intra_safe(q, kk, b2, vv, vb, levels, rowid, ones):
    C = q.shape[0]
    amat = jnp.zeros((C, C), F32)
    for hl, upper, same in levels:
        bref = _tile_row(b2, 2 * hl, hl - 1)
        qt = q * jnp.exp2(jnp.where(upper, b2 - bref, NEG_INF))
        kt = kk * jnp.exp2(jnp.where(upper, NEG_INF, bref - b2))
        amat = amat + jnp.where(same, _dot(qt.astype(BF16), kt.T.astype(BF16)), 0.0)
    o = _dot(amat.astype(BF16), vb)
    sub = rowid % HG_DIAG
    for s in range(HG_DIAG):
        dec = jnp.exp2(jnp.where(sub >= s, b2 - _tile_row(b2, HG_DIAG, s), NEG_INF))
        ps = q * _tile_row(kk, HG_DIAG, s) * dec
        o = o + _dot(ps.astype(BF16), ones) * _tile_row(vv, HG_DIAG, s)
    return o


def _hgrn_kernel(x_ref, s0_ref, g_ref, win_ref, wo_ref, gn_ref, lbp_ref,
                 o_ref, so_ref, st_ref, pj_ref, b2_ref, og_ref, *, chunk, nsub, layer):
    D = D_MODEL
    C = chunk
    t = pl.program_id(1)
    nt = pl.num_programs(1)

    @pl.when(t == 0)
    def _():
        for h in range(HG_HEADS):
            st_ref[h] = s0_ref[0, h].T

    x = x_ref[0]
    xg, rinv = _rms_split(x, g_ref[...])
    pj_ref[...] = _dot(xg, win_ref[...]) * rinv

    lbp = lbp_ref[...]
    e = jnp.exp(lbp - jnp.max(lbp, axis=0, keepdims=True))
    sm = e / jnp.sum(e, axis=0, keepdims=True)
    lb = jnp.zeros((1, D), F32)
    for i in range(1, layer + 1):
        lb = lb + sm[i:i + 1, :]
    log_lb = jnp.log(lb)
    log_1mlb = jnp.log1p(-lb)

    zf = pj_ref[:, D:2 * D]
    a = jnp.exp(-jnp.abs(zf))
    r = 1.0 / (1.0 + a)
    log_sig = jnp.minimum(zf, 0.0) - jnp.log(1.0 + a)
    bb = log_1mlb + log_sig
    logf = jnp.maximum(log_lb, bb) + jnp.log(1.0 + jnp.exp(-jnp.abs(log_lb - bb)))
    pj_ref[:, D:2 * D] = (1.0 - lb) * jnp.where(zf >= 0.0, a * r, r)
    zq = pj_ref[:, :D]
    pj_ref[:, :D] = zq * (1.0 / (1.0 + jnp.exp(-zq)))

    ri = lax.broadcasted_iota(jnp.int32, (C, C), 0)
    ci = lax.broadcasted_iota(jnp.int32, (C, C), 1)
    tril = ri >= ci
    tri = tril.astype(BF16)
    ends = []
    safe = None
    for c in range(nsub):
        hi, mid, lo = _split3(logf[c * C:(c + 1) * C, :])
        b2_c = (_dot(tri, hi) + _dot(tri, mid) + _dot(tri, lo)) * LOG2E
        b2_ref[c * C:(c + 1) * C, :] = b2_c
        b2_mid = b2_c[C // 2 - 1:C // 2, :]
        b2_end = b2_c[C - 1:C, :]
        ends.append(b2_end)
        ok = (jnp.min(b2_mid) >= -HG_SAFE_LOG2) & (jnp.min(b2_end - b2_mid) >= -HG_SAFE_LOG2)
        safe = ok if safe is None else safe & ok

    def finish(intra):
        for c in range(nsub):
            rows = slice(c * C, (c + 1) * C)
            for h in range(HG_HEADS):
                cols = slice(h * HG_DK, (h + 1) * HG_DK)
                q, kk, b2 = pj_ref[rows, cols], pj_ref[rows, D + h * HG_DK:D + (h + 1) * HG_DK], b2_ref[rows, cols]
                vv = pj_ref[rows, 2 * D + h * HG_DV:2 * D + (h + 1) * HG_DV]
                zg = pj_ref[rows, 3 * D + h * HG_DV:3 * D + (h + 1) * HG_DV]
                vb = vv.astype(BF16)
                st = st_ref[h]
                end = ends[c][:, cols]
                o = _dot((q * jnp.exp2(b2)).astype(BF16), st.T.astype(BF16))
                kdec = (kk * jnp.exp2(end - b2)).astype(BF16)
                st_ref[h] = st * jnp.exp2(end) + _dot_tn(vb, kdec)
                o = o + intra(q, kk, b2, vv, vb)
                on = o * lax.rsqrt(jnp.mean(o * o, axis=-1, keepdims=True) + EPS) * gn_ref[...]
                og_ref[rows, h * HG_DV:(h + 1) * HG_DV] = (on * (zg * (1.0 / (1.0 + jnp.exp(-zg))))).astype(BF16)
        o_ref[0] = x + _dot(og_ref[...], wo_ref[...])

    @pl.when(safe)
    def _():
        finish(lambda q, kk, b2, vv, vb: _hgrn_intra_fast(q, kk, b2, vb, tril))

    @pl.when(jnp.logical_not(safe))
    def _():
        rowid = lax.broadcasted_iota(jnp.int32, (C, 1), 0)
        ones = jnp.ones((HG_DK, HG_DV), BF16)
        levels = []
        hl = C // 2
        while hl >= HG_DIAG:
            levels.append((hl, (rowid % (2 * hl)) >= hl, (ri // (2 * hl)) == (ci // (2 * hl))))
            hl //= 2
        finish(lambda q, kk, b2, vv, vb: _hgrn_intra_safe(q, kk, b2, vv, vb, levels, rowid, ones))

    @pl.when(t == nt - 1)
    def _():
        for h in range(HG_HEADS):
            so_ref[0, h] = st_ref[h].T


def _hgrn(x, s0, g, w_in, w_o, gn, lbp, *, chunk, nsub, layer):
    S, L, D = x.shape
    tile = chunk * nsub
    assert L % tile == 0 and chunk % (2 * HG_DIAG) == 0
    tok = pl.BlockSpec((1, tile, D), lambda s, t: (s, t, 0))
    state = pl.BlockSpec((1, HG_HEADS, HG_DK, HG_DV), lambda s, t: (s, 0, 0, 0))
    return pl.pallas_call(
        functools.partial(_hgrn_kernel, chunk=chunk, nsub=nsub, layer=layer),
        grid=(S, L // tile),
        in_specs=[tok, state, _whole((1, D)), _whole((D, 4 * D)), _whole((D, D)), _whole((1, HG_DV)),
                  _whole((DEPTH, D))],
        out_specs=[tok, state],
        out_shape=[jax.ShapeDtypeStruct((S, L, D), F32),
                   jax.ShapeDtypeStruct((S, HG_HEADS, HG_DK, HG_DV), F32)],
        scratch_shapes=[pltpu.VMEM((HG_HEADS, HG_DV, HG_DK), F32),
                        pltpu.VMEM((tile, 4 * D), F32),
                        pltpu.VMEM((tile, D), F32),
                        pltpu.VMEM((tile, D), BF16)],
        compiler_params=_params(),
        name="hgrn",
    )(x, s0, g, w_in, w_o, gn, lbp)


MLP_TILE = 512
POOL_TILE = (512, 64)
ATTN_TILE = (128, 64)
ATTN_NSUB = (2, 1)
HGRN_CHUNK = (128, 64)
HGRN_NSUB = (4, 1)


def kernel(x_prompt, x_sample, cache_pool, cache_attn_k, cache_attn_v, state_hgrn, norm_mix, norm_ffn,
           norm_final, pool_w, pool_scale, attn_wqkv, attn_wo, attn_rel_bias, hgrn_w_in, hgrn_w_o,
           hgrn_gnorm, hgrn_lower_bounds, mlp_w1, mlp_w2):
    B, L, D = x_prompt.shape
    BS, LS, _ = x_sample.shape
    xp = x_prompt
    xs = x_sample
    gf = norm_final.reshape(1, D)
    w1_all = mlp_w1.astype(BF16)
    w2_all = mlp_w2.astype(BF16)
    pw_all = pool_w.astype(BF16)

    pool_p, pool_s, k_p, k_s, v_p, v_s, h_p, h_s = [], [], [], [], [], [], [], []
    for i in range(DEPTH):
        kind, j = i % N_MIXERS, i // N_MIXERS
        g = norm_mix[i].reshape(1, D)
        gm = norm_ffn[i].reshape(1, D)
        last = i == DEPTH - 1
        if kind == 0:
            sc = pool_scale[j].reshape(1, D)
            hist_p = jnp.zeros((B, POOL_HALO, D), F32)
            hist_s = jnp.pad(cache_pool[j], ((0, 0), (1, 0), (0, 0)))
            xs, cs = _pool(xs, hist_s, g, pw_all, sc, pool_layer=j, tile=POOL_TILE[1], p0=PAST_LEN)
            xp, cp, xs = _pool_mlp(xp, xs.reshape(1, BS * LS, D), hist_p, g, pw_all, sc, gm, w1_all, w2_all, gf,
                                   pool_layer=j, layer=i, tile=MLP_TILE, p0=0, final_norm=last)
            xs = xs.reshape(BS, LS, D)
            pool_p.append(cp)
            pool_s.append(cs)
        elif kind == 1:
            wqkv = attn_wqkv[j].astype(BF16)
            wo = attn_wo[j].astype(BF16)
            keep_p = min(ATTN_WINDOW, L)
            zeros = jnp.zeros((B, ATTN_WINDOW, D), F32)
            xp, kp, vp = _attn(xp, zeros, zeros, g, wqkv, wo, attn_rel_bias[j],
                               tile=ATTN_TILE[0], nsub=ATTN_NSUB[0], keep=keep_p, mask_start=True)
            ck = cache_attn_k[j].reshape(BS, ATTN_WINDOW, D)
            cv = cache_attn_v[j].reshape(BS, ATTN_WINDOW, D)
            xs, ks, vs = _attn(xs, ck, cv, g, wqkv, wo, attn_rel_bias[j],
                               tile=ATTN_TILE[1], nsub=ATTN_NSUB[1], keep=LS, mask_start=False)
            k_p.append(kp.reshape(B, keep_p, ATTN_HEADS, ATTN_HD))
            v_p.append(vp.reshape(B, keep_p, ATTN_HEADS, ATTN_HD))
            k_s.append(ks.reshape(BS, LS, ATTN_HEADS, ATTN_HD))
            v_s.append(vs.reshape(BS, LS, ATTN_HEADS, ATTN_HD))
        else:
            w_in = hgrn_w_in[j].astype(BF16)
            w_o = hgrn_w_o[j].astype(BF16)
            gn = hgrn_gnorm[j].reshape(1, HG_DV)
            s0 = jnp.zeros((B, HG_HEADS, HG_DK, HG_DV), F32)
            xp, sp = _hgrn(xp, s0, g, w_in, w_o, gn, hgrn_lower_bounds, chunk=HGRN_CHUNK[0], nsub=HGRN_NSUB[0],
                           layer=i)
            xs, ss = _hgrn(xs, state_hgrn[j], g, w_in, w_o, gn, hgrn_lower_bounds, chunk=HGRN_CHUNK[1],
                           nsub=HGRN_NSUB[1], layer=i)
            h_p.append(sp)
            h_s.append(ss)
        if kind != 0:
            xp, xs = _mlp(xp.reshape(1, B * L, D), xs.reshape(1, BS * LS, D), gm, w1_all, w2_all, gf, layer=i,
                          tile=MLP_TILE, final_norm=last)
            xp = xp.reshape(B, L, D)
            xs = xs.reshape(BS, LS, D)

    return (xp, xs, jnp.stack(pool_p), jnp.stack(pool_s), jnp.stack(k_p), jnp.stack(k_s),
            jnp.stack(v_p), jnp.stack(v_s), jnp.stack(h_p), jnp.stack(h_s))
```

```python
import functools
import math

import jax
import jax.numpy as jnp
from jax import lax
from jax.experimental import pallas as pl
from jax.experimental.pallas import tpu as pltpu

F32 = jnp.float32
BF16 = jnp.bfloat16

D_MODEL = 1024
DEPTH = 4
CHUNK = 64
N_MIXERS = 3
PAST_LEN = 2048
EPS = 1e-6
POOL_WINDOWS = (2, 4, 8, 16)
POOL_GROUPS = 4
POOL_GC = D_MODEL // POOL_GROUPS
POOL_HIST = max(POOL_WINDOWS) - 1
POOL_HALO = POOL_HIST + 1
ATTN_HEADS = 16
ATTN_HD = D_MODEL // ATTN_HEADS
N_PAST_CHUNKS = 8
ATTN_WINDOW = N_PAST_CHUNKS * CHUNK
REL_MIN = -(CHUNK - 1)
REL_MAX = 256
REL_SIZE = REL_MAX - REL_MIN + 1
HEAD_PAIRS = ATTN_HEADS // 2
PAIR_W = 2 * ATTN_HD
HG_HEADS = 8
HG_DK = D_MODEL // HG_HEADS
HG_DV = D_MODEL // HG_HEADS
HG_DIAG = 8
HG_SAFE_LOG2 = 100.0
LOG2E = math.log2(math.e)
D_FF = 4 * D_MODEL

V7X_VMEM_BYTES = 64 * 1024 * 1024
VMEM_LIMIT_BYTES = V7X_VMEM_BYTES - 8 * 1024 * 1024

NEG_INF = float("-inf")


def _params():
    return pltpu.CompilerParams(dimension_semantics=("arbitrary", "arbitrary"),
                                vmem_limit_bytes=VMEM_LIMIT_BYTES)


def _whole(shape, single_buffer=False):
    zeros = (0,) * len(shape)
    kw = dict(pipeline_mode=pl.Buffered(1)) if single_buffer else {}
    return pl.BlockSpec(shape, lambda s, t: zeros, **kw)


def _layer(shape, i):
    zeros = (0,) * len(shape)
    return pl.BlockSpec((None,) + tuple(shape), lambda s, t: (i,) + zeros)


def _cast_specs(rows, cols, nchunk, layer):
    assert rows % nchunk == 0 and (rows // nchunk) % 16 == 0
    cr = rows // nchunk
    src = pl.BlockSpec((None, cr, cols), lambda s, t: (layer, jnp.minimum(t, nchunk - 1), 0))
    dst = pl.BlockSpec((cr, cols), lambda s, t: (jnp.minimum(t, nchunk - 1), 0))
    return src, dst


def _rms(x, g):
    ms = jnp.mean(x * x, axis=-1, keepdims=True)
    return x * lax.rsqrt(ms + EPS) * g


def _rms_split(x, g):
    r = lax.rsqrt(jnp.mean(x * x, axis=-1, keepdims=True) + EPS)
    return (x * g).astype(BF16), r


def _dot(a, b):
    return jnp.dot(a, b, preferred_element_type=F32)


def _dot_tn(a, b):
    return lax.dot_general(a, b, (((0,), (0,)), ((), ())), preferred_element_type=F32)


def _split3(x):
    hi = x.astype(BF16)
    r1 = x - hi.astype(F32)
    mid = r1.astype(BF16)
    lo = (r1 - mid.astype(F32)).astype(BF16)
    return hi, mid, lo


def _mlp_apply(x, g, w1_ref, w2_ref, gf, final_norm):
    xg, r = _rms_split(x, g)
    h = _dot(xg, w1_ref[...])
    h = jnp.square(jnp.maximum(h, 0.0)).astype(BF16)
    y = x + (r * r) * _dot(h, w2_ref[...])
    if final_norm:
        y = _rms(y, gf)
    return y


def _mlp_kernel(xp_ref, xs_ref, g_ref, w1_ref, w2_ref, gf_ref, *rest, final_norm, cast_next):
    if cast_next:
        nw1_ref, nw2_ref, op_ref, os_ref, cw1_ref, cw2_ref = rest
        cw1_ref[...] = nw1_ref[...].astype(BF16)
        cw2_ref[...] = nw2_ref[...].astype(BF16)
    else:
        op_ref, os_ref = rest
    t = pl.program_id(1)
    last = pl.num_programs(1) - 1

    @pl.when(t < last)
    def _():
        op_ref[0] = _mlp_apply(xp_ref[0], g_ref[...], w1_ref, w2_ref, gf_ref[...], final_norm)

    @pl.when(t == last)
    def _():
        os_ref[0] = _mlp_apply(xs_ref[0], g_ref[...], w1_ref, w2_ref, gf_ref[...], final_norm)


def _mlp(xp, xs, g, w1, w2, gf, w1_f32, w2_f32, *, layer, tile, final_norm):
    _, Lp, D = xp.shape
    _, Ls, _ = xs.shape
    ntp = Lp // tile
    assert Lp % tile == 0
    cast_next = layer + 1 < DEPTH
    tok = pl.BlockSpec((1, tile, D), lambda s, t: (0, jnp.minimum(t, ntp - 1), 0))
    smp = pl.BlockSpec((1, Ls, D), lambda s, t: (0, 0, 0))
    in_specs = [tok, smp, _whole((1, D)), _whole((D, D_FF)), _whole((D_FF, D)), _whole((1, D))]
    out_specs = [tok, smp]
    out_shape = [jax.ShapeDtypeStruct((1, Lp, D), F32), jax.ShapeDtypeStruct((1, Ls, D), F32)]
    args = [xp, xs, g, w1, w2, gf]
    if cast_next:
        s1, d1 = _cast_specs(D, D_FF, ntp, layer + 1)
        s2, d2 = _cast_specs(D_FF, D, ntp, layer + 1)
        in_specs += [s1, s2]
        out_specs += [d1, d2]
        out_shape += [jax.ShapeDtypeStruct((D, D_FF), BF16), jax.ShapeDtypeStruct((D_FF, D), BF16)]
        args += [w1_f32, w2_f32]
    outs = pl.pallas_call(
        functools.partial(_mlp_kernel, final_norm=final_norm, cast_next=cast_next),
        grid=(1, ntp + 1),
        in_specs=in_specs,
        out_specs=out_specs,
        out_shape=out_shape,
        compiler_params=_params(),
        name="mlp",
    )(*args)
    return (outs[0], outs[1], outs[2], outs[3]) if cast_next else (outs[0], outs[1], None, None)


def _pool_group(u, pos, ext_ref, w_ref, gi, tile):
    w = POOL_WINDOWS[gi]
    c0, c1 = gi * POOL_GC, (gi + 1) * POOL_GC
    win = u[:, c0:c1]
    for j in range(1, w):
        win = win + ext_ref[POOL_HALO - j:POOL_HALO - j + tile, c0:c1]
    cnt = jnp.minimum(pos + 1, w).astype(F32)
    diff = win / cnt - u[:, c0:c1]
    return _dot(diff.astype(BF16), w_ref[gi])


def _pool_apply(x, pos0, ext_ref, g, w_ref, sc, tile):
    u = _rms(x, g)
    ext_ref[POOL_HALO:POOL_HALO + tile, :] = u
    pos = pos0 + lax.broadcasted_iota(jnp.int32, (tile, 1), 0)
    ys = [_pool_group(u, pos, ext_ref, w_ref, gi, tile) for gi in range(POOL_GROUPS)]
    y = jnp.concatenate(ys, axis=1) * sc
    return x + y, ext_ref[tile:tile + POOL_HALO, :]


def _pool_kernel(x_ref, hist_ref, g_ref, w_ref, sc_ref, o_ref, cache_ref, ext_ref, *, tile, p0):
    t = pl.program_id(1)

    @pl.when(t == 0)
    def _():
        ext_ref[0:POOL_HALO, :] = hist_ref[0]

    out, tail = _pool_apply(x_ref[0], p0 + t * tile, ext_ref, g_ref[...], w_ref, sc_ref[...], tile)
    o_ref[0] = out
    cache_ref[0] = tail
    ext_ref[0:POOL_HALO, :] = tail


def _pool_mlp_kernel(x0_ref, xn_ref, xs_ref, hist_ref, gm_ref, pw_ref, sc_ref, gffn_ref, w1_ref, w2_ref, gf_ref,
                     *rest, tile, p0, final_norm, cast_next):
    if cast_next:
        nw1_ref, nw2_ref, o_ref, cache_ref, os_ref, cw1_ref, cw2_ref, ext_ref, x1_ref = rest
        cw1_ref[...] = nw1_ref[...].astype(BF16)
        cw2_ref[...] = nw2_ref[...].astype(BF16)
    else:
        o_ref, cache_ref, os_ref, ext_ref, x1_ref = rest
    t = pl.program_id(1)
    nt = pl.num_programs(1) - 1

    @pl.when(t == nt)
    def _():
        os_ref[0] = _mlp_apply(xs_ref[0], gffn_ref[...], w1_ref, w2_ref, gf_ref[...], final_norm)

    @pl.when(t == 0)
    def _():
        ext_ref[0:POOL_HALO, :] = hist_ref[0]
        out, tail = _pool_apply(x0_ref[0], p0, ext_ref, gm_ref[...], pw_ref, sc_ref[...], tile)
        x1_ref[0] = out
        ext_ref[0:POOL_HALO, :] = tail

    @pl.when(t < nt)
    def _():
        x1 = x1_ref[lax.rem(t, 2)]
        v, r = _rms_split(x1, gffn_ref[...])
        halo = ext_ref[0:POOL_HALO, :]
        xn = xn_ref[0]
        u = _rms(xn, gm_ref[...])
        ext_ref[POOL_HALO:POOL_HALO + tile, :] = u
        pos = p0 + (t + 1) * tile + lax.broadcasted_iota(jnp.int32, (tile, 1), 0)
        fc = D_FF // POOL_GROUPS
        acc = None
        ys = []
        for c in range(POOL_GROUPS):
            h = _dot(v, w1_ref[:, c * fc:(c + 1) * fc])
            h = jnp.square(jnp.maximum(h, 0.0)).astype(BF16)
            part = _dot(h, w2_ref[c * fc:(c + 1) * fc, :])
            acc = part if acc is None else acc + part
            ys.append(_pool_group(u, pos, ext_ref, pw_ref, c, tile))
        acc = x1 + (r * r) * acc
        if final_norm:
            acc = _rms(acc, gf_ref[...])
        o_ref[0] = acc

        x1_ref[lax.rem(t + 1, 2)] = xn + jnp.concatenate(ys, axis=1) * sc_ref[...]
        halo = jnp.where(t + 1 < nt, ext_ref[tile:tile + POOL_HALO, :], halo)
        ext_ref[0:POOL_HALO, :] = halo
        cache_ref[0] = halo


def _pool_mlp(x, xs, hist, gm, pw, sc, gffn, w1, w2, gf, w1_f32, w2_f32, *, pool_layer, layer, tile, p0,
              final_norm):
    S, L, D = x.shape
    _, Ls, _ = xs.shape
    nt = L // tile
    assert S == 1 and L % tile == 0 and tile >= POOL_HALO
    cast_next = layer + 1 < DEPTH
    first = pl.BlockSpec((1, tile, D), lambda s, t: (s, 0, 0))
    nxt = pl.BlockSpec((1, tile, D), lambda s, t: (s, jnp.minimum(t + 1, nt - 1), 0))
    tok = pl.BlockSpec((1, tile, D), lambda s, t: (s, jnp.minimum(t, nt - 1), 0))
    smp = pl.BlockSpec((1, Ls, D), lambda s, t: (0, 0, 0))
    seq = pl.BlockSpec((1, POOL_HALO, D), lambda s, t: (s, 0, 0))
    in_specs = [first, nxt, smp, seq, _whole((1, D)), _layer((POOL_GROUPS, POOL_GC, POOL_GC), pool_layer),
                _whole((1, D)), _whole((1, D)),
                _whole((D, D_FF), single_buffer=True), _whole((D_FF, D), single_buffer=True), _whole((1, D))]
    out_specs = [tok, seq, smp]
    out_shape = [jax.ShapeDtypeStruct((S, L, D), F32), jax.ShapeDtypeStruct((S, POOL_HALO, D), F32),
                 jax.ShapeDtypeStruct((1, Ls, D), F32)]
    args = [x, x, xs, hist, gm, pw, sc, gffn, w1, w2, gf]
    if cast_next:
        s1, d1 = _cast_specs(D, D_FF, nt, layer + 1)
        s2, d2 = _cast_specs(D_FF, D, nt, layer + 1)
        in_specs += [s1, s2]
        out_specs += [d1, d2]
        out_shape += [jax.ShapeDtypeStruct((D, D_FF), BF16), jax.ShapeDtypeStruct((D_FF, D), BF16)]
        args += [w1_f32, w2_f32]
    outs = pl.pallas_call(
        functools.partial(_pool_mlp_kernel, tile=tile, p0=p0, final_norm=final_norm, cast_next=cast_next),
        grid=(S, nt + 1),
        in_specs=in_specs,
        out_specs=out_specs,
        out_shape=out_shape,
        scratch_shapes=[pltpu.VMEM((POOL_HALO + tile, D), F32), pltpu.VMEM((2, tile, D), F32)],
        compiler_params=_params(),
        name="pool_mlp",
    )(*args)
    nw1, nw2 = (outs[3], outs[4]) if cast_next else (None, None)
    return outs[0], outs[1][:, 1:, :], outs[2], nw1, nw2


def _pool(x, hist, g, w, sc, *, pool_layer, tile, p0):
    S, L, D = x.shape
    assert L % tile == 0 and tile >= POOL_HALO
    tok = pl.BlockSpec((1, tile, D), lambda s, t: (s, t, 0))
    seq = pl.BlockSpec((1, POOL_HALO, D), lambda s, t: (s, 0, 0))
    out, cache = pl.pallas_call(
        functools.partial(_pool_kernel, tile=tile, p0=p0),
        grid=(S, L // tile),
        in_specs=[tok, seq, _whole((1, D)), _layer((POOL_GROUPS, POOL_GC, POOL_GC), pool_layer), _whole((1, D))],
        out_specs=[tok, seq],
        out_shape=[jax.ShapeDtypeStruct((S, L, D), F32), jax.ShapeDtypeStruct((S, POOL_HALO, D), F32)],
        scratch_shapes=[pltpu.VMEM((POOL_HALO + tile, D), F32)],
        compiler_params=_params(),
        name="pool",
    )(x, hist, g, w, sc)
    return out, cache[:, 1:, :]


def _build_attn_bias(rb_ref, bias_ref, *, tile):
    nb = ATTN_WINDOW // tile + 1
    w2 = 2 * tile
    col = lax.broadcasted_iota(jnp.int32, (REL_SIZE, nb * w2), 1)
    row = lax.broadcasted_iota(jnp.int32, (REL_SIZE, nb * w2), 0)
    ip = col % w2
    i = jnp.where(ip < tile, ip, ip - w2)
    idx = jnp.clip((col // w2) * tile - i, REL_MIN, REL_MAX) - REL_MIN
    sel = jnp.where(row == idx, 1.0, 0.0).astype(BF16)
    hi, mid, lo = _split3(rb_ref[...])
    gen = _dot(hi, sel) + _dot(mid, sel) + _dot(lo, sel)

    r = lax.broadcasted_iota(jnp.int32, (tile, w2), 0)
    lane = lax.broadcasted_iota(jnp.int32, (tile, w2), 1)
    first = lane < tile
    for d in range(nb):
        dchunk = (d * tile) // CHUNK + r // CHUNK - (lane % tile) // CHUNK
        ok = (dchunk >= 0) & (dchunk <= N_PAST_CHUNKS)
        for p in range(HEAD_PAIRS):
            ga = jnp.broadcast_to(gen[2 * p:2 * p + 1, d * w2:(d + 1) * w2], (tile, w2))
            gb = jnp.broadcast_to(gen[2 * p + 1:2 * p + 2, d * w2:(d + 1) * w2], (tile, w2))
            ta = pltpu.roll(ga, 0, 1, stride=1, stride_axis=0)
            tb = pltpu.roll(gb, tile, 1, stride=1, stride_axis=0)
            bias_ref[p, d] = jnp.where(ok, jnp.where(first, ta, tb), NEG_INF)


def _attn_kernel(x_ref, hk_ref, hv_ref, g_ref, wqkv_ref, wo_ref, rb_ref,
                 o_ref, ko_ref, vo_ref, kt_ring, v_ring, bias_ref, *, tile, nsub, mask_start):
    D = D_MODEL
    nb = ATTN_WINDOW // tile + 1
    ring = ATTN_WINDOW // tile + nsub
    t = pl.program_id(1)

    row_lo = lax.broadcasted_iota(jnp.int32, (PAIR_W, tile), 0) < ATTN_HD
    lane_lo = lax.broadcasted_iota(jnp.int32, (tile, PAIR_W), 1) < ATTN_HD
    first = lax.broadcasted_iota(jnp.int32, (tile, 2 * tile), 1) < tile

    @pl.when((pl.program_id(0) == 0) & (t == 0))
    def _():
        _build_attn_bias(rb_ref, bias_ref, tile=tile)

    def store_block(slot, k, v):
        kt = k.T.astype(BF16)
        vb = v.astype(BF16)
        zk = jnp.zeros((PAIR_W, tile), BF16)
        zv = jnp.zeros((tile, PAIR_W), BF16)
        for p in range(HEAD_PAIRS):
            ktp = kt[p * PAIR_W:(p + 1) * PAIR_W, :]
            kt_ring[slot, p] = jnp.concatenate(
                [jnp.where(row_lo, ktp, zk), jnp.where(row_lo, zk, ktp)], axis=1)
            vp = vb[:, p * PAIR_W:(p + 1) * PAIR_W]
            v_ring[slot, p] = jnp.concatenate(
                [jnp.where(lane_lo, vp, zv), jnp.where(lane_lo, zv, vp)], axis=0)

    @pl.when(t == 0)
    def _():
        for i in range(nb - 1):
            store_block(i + nsub, hk_ref[0, i * tile:(i + 1) * tile, :], hv_ref[0, i * tile:(i + 1) * tile, :])

    x = x_ref[0]
    xg, r = _rms_split(x, g_ref[...])
    qkv = _dot(xg, wqkv_ref[...])
    q = (qkv[:, :D] * (r * (ATTN_HD ** -0.5))).astype(BF16)
    k = qkv[:, D:2 * D] * r
    v = qkv[:, 2 * D:] * r
    ko_ref[0] = k
    vo_ref[0] = v
    for j in range(nsub):
        store_block(lax.rem(t * nsub + j, ring), k[j * tile:(j + 1) * tile, :], v[j * tile:(j + 1) * tile, :])

    def slot_of(j, d):
        return lax.rem(t * nsub + j + ring - d, ring)

    def score_block(j, p, d):
        s = _dot(q[j * tile:(j + 1) * tile, p * PAIR_W:(p + 1) * PAIR_W], kt_ring[slot_of(j, d), p]) + bias_ref[p, d]
        if mask_start and d > j:
            s = s + jnp.where(t * nsub + j >= d, 0.0, NEG_INF)
        return s

    def row_max(scores):
        mx = functools.reduce(jnp.maximum, scores)
        m_a = jnp.max(mx[:, :tile], axis=-1, keepdims=True)
        m_b = jnp.max(mx[:, tile:], axis=-1, keepdims=True)
        return jnp.where(first, m_a, m_b)

    units = [(j, p) for j in range(nsub) for p in range(HEAD_PAIRS)]
    outs = [[] for _ in range(nsub)]
    nxt = [score_block(0, 0, d) for d in range(nb)]
    for n, (j, p) in enumerate(units):
        scores, m = nxt, row_max(nxt)
        if n + 1 < len(units):
            nxt = [score_block(units[n + 1][0], units[n + 1][1], d) for d in range(nb)]
        lsum = jnp.zeros((tile, 2 * tile), F32)
        acc = jnp.zeros((tile, PAIR_W), F32)
        for d in range(nb):
            pr = jnp.exp(scores[d] - m)
            lsum = lsum + pr
            acc = acc + _dot(pr.astype(BF16), v_ring[slot_of(j, d), p])
        l_a = jnp.sum(lsum[:, :tile], axis=-1, keepdims=True)
        l_b = jnp.sum(lsum[:, tile:], axis=-1, keepdims=True)
        inv = jnp.where(lane_lo, 1.0 / l_a, 1.0 / l_b)
        outs[j].append((acc * inv).astype(BF16))
    o = jnp.concatenate([jnp.concatenate(row, axis=1) for row in outs], axis=0)
    o_ref[0] = x + _dot(o, wo_ref[...])


def _attn(x, hist_k, hist_v, g, wqkv, wo, rel_bias, *, tile, nsub, keep, mask_start):
    S, L, D = x.shape
    step = tile * nsub
    nb = ATTN_WINDOW // tile + 1
    nt = L // step
    nkeep = keep // step
    assert L % step == 0 and tile % CHUNK == 0 and ATTN_WINDOW % tile == 0 and keep % step == 0
    tok = pl.BlockSpec((1, step, D), lambda s, t: (s, t, 0))
    hist = pl.BlockSpec((1, ATTN_WINDOW, D), lambda s, t: (s, 0, 0))
    kv_out = pl.BlockSpec((1, step, D), lambda s, t: (s, jnp.maximum(t - (nt - nkeep), 0), 0))
    return pl.pallas_call(
        functools.partial(_attn_kernel, tile=tile, nsub=nsub, mask_start=mask_start),
        grid=(S, nt),
        in_specs=[tok, hist, hist, _whole((1, D)), _whole((D, 3 * D)), _whole((D, D)),
                  _whole((ATTN_HEADS, REL_SIZE))],
        out_specs=[tok, kv_out, kv_out],
        out_shape=[jax.ShapeDtypeStruct((S, L, D), F32),
                   jax.ShapeDtypeStruct((S, keep, D), F32),
                   jax.ShapeDtypeStruct((S, keep, D), F32)],
        scratch_shapes=[pltpu.VMEM((nb - 1 + nsub, HEAD_PAIRS, PAIR_W, 2 * tile), BF16),
                        pltpu.VMEM((nb - 1 + nsub, HEAD_PAIRS, 2 * tile, PAIR_W), BF16),
                        pltpu.VMEM((HEAD_PAIRS, nb, tile, 2 * tile), F32)],
        compiler_params=_params(),
        name="attn",
    )(x, hist_k, hist_v, g, wqkv, wo, rel_bias)


def _tile_row(x, n, s):
    C, W = x.shape
    xr = x.reshape(C // n, n, W)
    return jnp.broadcast_to(xr[:, s:s + 1, :], (C // n, n, W)).reshape(C, W)


def _hgrn_intra_fast(q, kk, b2, vb, tril):
    C = q.shape[0]
    bmid = b2[C // 2 - 1:C // 2, :]
    qt = (q * jnp.exp2(b2 - bmid)).astype(BF16)
    kt = (kk * jnp.exp2(bmid - b2)).T.astype(BF16)
    amat = jnp.where(tril, _dot(qt, kt), 0.0)
    return _dot(amat.astype(BF16), vb)


def _hgrn_intra_safe(q, kk, b2, vv, vb, levels, rowid, ones):
    C = q.shape[0]
    amat = jnp.zeros((C, C), F32)
    for hl, upper, same in levels:
        bref = _tile_row(b2, 2 * hl, hl - 1)
        qt = q * jnp.exp2(jnp.where(upper, b2 - bref, NEG_INF))
        kt = kk * jnp.exp2(jnp.where(upper, NEG_INF, bref - b2))
        amat = amat + jnp.where(same, _dot(qt.astype(BF16), kt.T.astype(BF16)), 0.0)
    o = _dot(amat.astype(BF16), vb)
    sub = rowid % HG_DIAG
    for s in range(HG_DIAG):
        dec = jnp.exp2(jnp.where(sub >= s, b2 - _tile_row(b2, HG_DIAG, s), NEG_INF))
        ps = q * _tile_row(kk, HG_DIAG, s) * dec
        o = o + _dot(ps.astype(BF16), ones) * _tile_row(vv, HG_DIAG, s)
    return o


def _hgrn_kernel(x_ref, s0_ref, g_ref, win_ref, wo_ref, gn_ref, lbp_ref,
                 o_ref, so_ref, st_ref, pj_ref, b2_ref, og_ref, *, chunk, nsub, layer):
    D = D_MODEL
    C = chunk
    t = pl.program_id(1)
    nt = pl.num_programs(1)

    @pl.when(t == 0)
    def _():
        for h in range(HG_HEADS):
            st_ref[h] = s0_ref[0, h].T

    x = x_ref[0]
    xg, rinv = _rms_split(x, g_ref[...])
    pj_ref[...] = _dot(xg, win_ref[...]) * rinv

    lbp = lbp_ref[...]
    e = jnp.exp(lbp - jnp.max(lbp, axis=0, keepdims=True))
    sm = e / jnp.sum(e, axis=0, keepdims=True)
    lb = jnp.zeros((1, D), F32)
    for i in range(1, layer + 1):
        lb = lb + sm[i:i + 1, :]
    log_lb = jnp.log(lb)
    log_1mlb = jnp.log1p(-lb)

    zf = pj_ref[:, D:2 * D]
    a = jnp.exp(-jnp.abs(zf))
    r = 1.0 / (1.0 + a)
    log_sig = jnp.minimum(zf, 0.0) - jnp.log(1.0 + a)
    bb = log_1mlb + log_sig
    logf = jnp.maximum(log_lb, bb) + jnp.log(1.0 + jnp.exp(-jnp.abs(log_lb - bb)))
    pj_ref[:, D:2 * D] = (1.0 - lb) * jnp.where(zf >= 0.0, a * r, r)
    zq = pj_ref[:, :D]
    pj_ref[:, :D] = zq * (1.0 / (1.0 + jnp.exp(-zq)))

    ri = lax.broadcasted_iota(jnp.int32, (C, C), 0)
    ci = lax.broadcasted_iota(jnp.int32, (C, C), 1)
    tril = ri >= ci
    tri = tril.astype(BF16)
    ends = []
    safe = None
    for c in range(nsub):
        hi, mid, lo = _split3(logf[c * C:(c + 1) * C, :])
        b2_c = (_dot(tri, hi) + _dot(tri, mid) + _dot(tri, lo)) * LOG2E
        b2_ref[c * C:(c + 1) * C, :] = b2_c
        b2_mid = b2_c[C // 2 - 1:C // 2, :]
        b2_end = b2_c[C - 1:C, :]
        ends.append(b2_end)
        ok = (jnp.min(b2_mid) >= -HG_SAFE_LOG2) & (jnp.min(b2_end - b2_mid) >= -HG_SAFE_LOG2)
        safe = ok if safe is None else safe & ok

    def finish(intra):
        for c in range(nsub):
            rows = slice(c * C, (c + 1) * C)
            for h in range(HG_HEADS):
                cols = slice(h * HG_DK, (h + 1) * HG_DK)
                q, kk, b2 = pj_ref[rows, cols], pj_ref[rows, D + h * HG_DK:D + (h + 1) * HG_DK], b2_ref[rows, cols]
                vv = pj_ref[rows, 2 * D + h * HG_DV:2 * D + (h + 1) * HG_DV]
                zg = pj_ref[rows, 3 * D + h * HG_DV:3 * D + (h + 1) * HG_DV]
                vb = vv.astype(BF16)
                st = st_ref[h]
                end = ends[c][:, cols]
                o = _dot((q * jnp.exp2(b2)).astype(BF16), st.T.astype(BF16))
                kdec = (kk * jnp.exp2(end - b2)).astype(BF16)
                st_ref[h] = st * jnp.exp2(end) + _dot_tn(vb, kdec)
                o = o + intra(q, kk, b2, vv, vb)
                on = o * lax.rsqrt(jnp.mean(o * o, axis=-1, keepdims=True) + EPS) * gn_ref[...]
                og_ref[rows, h * HG_DV:(h + 1) * HG_DV] = (on * (zg * (1.0 / (1.0 + jnp.exp(-zg))))).astype(BF16)
        o_ref[0] = x + _dot(og_ref[...], wo_ref[...])

    @pl.when(safe)
    def _():
        finish(lambda q, kk, b2, vv, vb: _hgrn_intra_fast(q, kk, b2, vb, tril))

    @pl.when(jnp.logical_not(safe))
    def _():
        rowid = lax.broadcasted_iota(jnp.int32, (C, 1), 0)
        ones = jnp.ones((HG_DK, HG_DV), BF16)
        levels = []
        hl = C // 2
        while hl >= HG_DIAG:
            levels.append((hl, (rowid % (2 * hl)) >= hl, (ri // (2 * hl)) == (ci // (2 * hl))))
            hl //= 2
        finish(lambda q, kk, b2, vv, vb: _hgrn_intra_safe(q, kk, b2, vv, vb, levels, rowid, ones))

    @pl.when(t == nt - 1)
    def _():
        for h in range(HG_HEADS):
            so_ref[0, h] = st_ref[h].T


def _hgrn(x, s0, g, w_in, w_o, gn, lbp, *, chunk, nsub, layer):
    S, L, D = x.shape
    tile = chunk * nsub
    assert L % tile == 0 and chunk % (2 * HG_DIAG) == 0
    tok = pl.BlockSpec((1, tile, D), lambda s, t: (s, t, 0))
    state = pl.BlockSpec((1, HG_HEADS, HG_DK, HG_DV), lambda s, t: (s, 0, 0, 0))
    return pl.pallas_call(
        functools.partial(_hgrn_kernel, chunk=chunk, nsub=nsub, layer=layer),
        grid=(S, L // tile),
        in_specs=[tok, state, _whole((1, D)), _whole((D, 4 * D)), _whole((D, D)), _whole((1, HG_DV)),
                  _whole((DEPTH, D))],
        out_specs=[tok, state],
        out_shape=[jax.ShapeDtypeStruct((S, L, D), F32),
                   jax.ShapeDtypeStruct((S, HG_HEADS, HG_DK, HG_DV), F32)],
        scratch_shapes=[pltpu.VMEM((HG_HEADS, HG_DV, HG_DK), F32),
                        pltpu.VMEM((tile, 4 * D), F32),
                        pltpu.VMEM((tile, D), F32),
                        pltpu.VMEM((tile, D), BF16)],
        compiler_params=_params(),
        name="hgrn",
    )(x, s0, g, w_in, w_o, gn, lbp)


MLP_TILE = 512
POOL_TILE = (512, 64)
ATTN_TILE = (128, 64)
ATTN_NSUB = (2, 1)
HGRN_CHUNK = (128, 64)
HGRN_NSUB = (4, 1)


def kernel(x_prompt, x_sample, cache_pool, cache_attn_k, cache_attn_v, state_hgrn, norm_mix, norm_ffn,
           norm_final, pool_w, pool_scale, attn_wqkv, attn_wo, attn_rel_bias, hgrn_w_in, hgrn_w_o,
           hgrn_gnorm, hgrn_lower_bounds, mlp_w1, mlp_w2):
    B, L, D = x_prompt.shape
    BS, LS, _ = x_sample.shape
    xp = x_prompt
    xs = x_sample
    gf = norm_final.reshape(1, D)
    w1 = mlp_w1[0].astype(BF16)
    w2 = mlp_w2[0].astype(BF16)
    pw_all = pool_w.astype(BF16)

    pool_p, pool_s, k_p, k_s, v_p, v_s, h_p, h_s = [], [], [], [], [], [], [], []
    for i in range(DEPTH):
        kind, j = i % N_MIXERS, i // N_MIXERS
        g = norm_mix[i].reshape(1, D)
        gm = norm_ffn[i].reshape(1, D)
        last = i == DEPTH - 1
        if kind == 0:
            sc = pool_scale[j].reshape(1, D)
            hist_p = jnp.zeros((B, POOL_HALO, D), F32)
            hist_s = jnp.pad(cache_pool[j], ((0, 0), (1, 0), (0, 0)))
            xs, cs = _pool(xs, hist_s, g, pw_all, sc, pool_layer=j, tile=POOL_TILE[1], p0=PAST_LEN)
            xp, cp, xs, w1, w2 = _pool_mlp(xp, xs.reshape(1, BS * LS, D), hist_p, g, pw_all, sc, gm, w1, w2, gf,
                                           mlp_w1, mlp_w2, pool_layer=j, layer=i, tile=MLP_TILE, p0=0,
                                           final_norm=last)
            xs = xs.reshape(BS, LS, D)
            pool_p.append(cp)
            pool_s.append(cs)
        elif kind == 1:
            wqkv = attn_wqkv[j].astype(BF16)
            wo = attn_wo[j].astype(BF16)
            keep_p = min(ATTN_WINDOW, L)
            zeros = jnp.zeros((B, ATTN_WINDOW, D), F32)
            xp, kp, vp = _attn(xp, zeros, zeros, g, wqkv, wo, attn_rel_bias[j],
                               tile=ATTN_TILE[0], nsub=ATTN_NSUB[0], keep=keep_p, mask_start=True)
            ck = cache_attn_k[j].reshape(BS, ATTN_WINDOW, D)
            cv = cache_attn_v[j].reshape(BS, ATTN_WINDOW, D)
            xs, ks, vs = _attn(xs, ck, cv, g, wqkv, wo, attn_rel_bias[j],
                               tile=ATTN_TILE[1], nsub=ATTN_NSUB[1], keep=LS, mask_start=False)
            k_p.append(kp.reshape(B, keep_p, ATTN_HEADS, ATTN_HD))
            v_p.append(vp.reshape(B, keep_p, ATTN_HEADS, ATTN_HD))
            k_s.append(ks.reshape(BS, LS, ATTN_HEADS, ATTN_HD))
            v_s.append(vs.reshape(BS, LS, ATTN_HEADS, ATTN_HD))
        else:
            w_in = hgrn_w_in[j].astype(BF16)
            w_o = hgrn_w_o[j].astype(BF16)
            gn = hgrn_gnorm[j].reshape(1, HG_DV)
            s0 = jnp.zeros((B, HG_HEADS, HG_DK, HG_DV), F32)
            xp, sp = _hgrn(xp, s0, g, w_in, w_o, gn, hgrn_lower_bounds, chunk=HGRN_CHUNK[0], nsub=HGRN_NSUB[0],
                           layer=i)
            xs, ss = _hgrn(xs, state_hgrn[j], g, w_in, w_o, gn, hgrn_lower_bounds, chunk=HGRN_CHUNK[1],
                           nsub=HGRN_NSUB[1], layer=i)
            h_p.append(sp)
            h_s.append(ss)
        if kind != 0:
            xp, xs, w1, w2 = _mlp(xp.reshape(1, B * L, D), xs.reshape(1, BS * LS, D), gm, w1, w2, gf,
                                  mlp_w1, mlp_w2, layer=i, tile=MLP_TILE, final_norm=last)
            xp = xp.reshape(B, L, D)
            xs = xs.reshape(BS, LS, D)

    return (xp, xs, jnp.stack(pool_p), jnp.stack(pool_s), jnp.stack(k_p), jnp.stack(k_s),
            jnp.stack(v_p), jnp.stack(v_s), jnp.stack(h_p), jnp.stack(h_s))
```

```python
import functools
import math

import jax
import jax.numpy as jnp
from jax import lax
from jax.experimental import pallas as pl
from jax.experimental.pallas import tpu as pltpu

F32 = jnp.float32
BF16 = jnp.bfloat16

D_MODEL = 1024
DEPTH = 4
CHUNK = 64
N_MIXERS = 3
PAST_LEN = 2048
EPS = 1e-6
POOL_WINDOWS = (2, 4, 8, 16)
POOL_GROUPS = 4
POOL_GC = D_MODEL // POOL_GROUPS
POOL_HIST = max(POOL_WINDOWS) - 1
POOL_HALO = POOL_HIST + 1
ATTN_HEADS = 16
ATTN_HD = D_MODEL // ATTN_HEADS
N_PAST_CHUNKS = 8
ATTN_WINDOW = N_PAST_CHUNKS * CHUNK
REL_MIN = -(CHUNK - 1)
REL_MAX = 256
REL_SIZE = REL_MAX - REL_MIN + 1
HEAD_PAIRS = ATTN_HEADS // 2
PAIR_W = 2 * ATTN_HD
HG_HEADS = 8
HG_DK = D_MODEL // HG_HEADS
HG_DV = D_MODEL // HG_HEADS
HG_DIAG = 8
HG_SAFE_LOG2 = 100.0
LOG2E = math.log2(math.e)
D_FF = 4 * D_MODEL

V7X_VMEM_BYTES = 64 * 1024 * 1024
VMEM_LIMIT_BYTES = V7X_VMEM_BYTES - 8 * 1024 * 1024

NEG_INF = float("-inf")


def _params():
    return pltpu.CompilerParams(dimension_semantics=("arbitrary", "arbitrary"),
                                vmem_limit_bytes=VMEM_LIMIT_BYTES)


def _whole(shape, single_buffer=False):
    zeros = (0,) * len(shape)
    kw = dict(pipeline_mode=pl.Buffered(1)) if single_buffer else {}
    return pl.BlockSpec(shape, lambda s, t: zeros, **kw)


def _layer(shape, i):
    zeros = (0,) * len(shape)
    return pl.BlockSpec((None,) + tuple(shape), lambda s, t: (i,) + zeros)


def _cast_specs(w, layer, nchunk):
    _, rows, cols = w.shape
    assert rows % nchunk == 0 and (rows // nchunk) % 16 == 0
    cr = rows // nchunk
    src = pl.BlockSpec((None, cr, cols), lambda s, t: (layer, jnp.minimum(t, nchunk - 1), 0))
    dst = pl.BlockSpec((cr, cols), lambda s, t: (jnp.minimum(t, nchunk - 1), 0))
    return src, dst, jax.ShapeDtypeStruct((rows, cols), BF16)


def _cast_chunks(src_refs, dst_refs):
    for s_ref, d_ref in zip(src_refs, dst_refs):
        d_ref[...] = s_ref[...].astype(BF16)


def _rms(x, g):
    ms = jnp.mean(x * x, axis=-1, keepdims=True)
    return x * lax.rsqrt(ms + EPS) * g


def _rms_split(x, g):
    r = lax.rsqrt(jnp.mean(x * x, axis=-1, keepdims=True) + EPS)
    return (x * g).astype(BF16), r


def _dot(a, b):
    return jnp.dot(a, b, preferred_element_type=F32)


def _dot_tn(a, b):
    return lax.dot_general(a, b, (((0,), (0,)), ((), ())), preferred_element_type=F32)


def _split3(x):
    hi = x.astype(BF16)
    r1 = x - hi.astype(F32)
    mid = r1.astype(BF16)
    lo = (r1 - mid.astype(F32)).astype(BF16)
    return hi, mid, lo


def _mlp_apply(x, g, w1_ref, w2_ref, gf, final_norm):
    xg, r = _rms_split(x, g)
    h = _dot(xg, w1_ref[...])
    h = jnp.square(jnp.maximum(h, 0.0)).astype(BF16)
    y = x + (r * r) * _dot(h, w2_ref[...])
    if final_norm:
        y = _rms(y, gf)
    return y


def _mlp_kernel(xp_ref, xs_ref, g_ref, w1_ref, w2_ref, gf_ref, *rest, final_norm, n_cast):
    op_ref, os_ref = rest[n_cast:n_cast + 2]
    _cast_chunks(rest[:n_cast], rest[n_cast + 2:])
    t = pl.program_id(1)
    last = pl.num_programs(1) - 1

    @pl.when(t < last)
    def _():
        op_ref[0] = _mlp_apply(xp_ref[0], g_ref[...], w1_ref, w2_ref, gf_ref[...], final_norm)

    @pl.when(t == last)
    def _():
        os_ref[0] = _mlp_apply(xs_ref[0], g_ref[...], w1_ref, w2_ref, gf_ref[...], final_norm)


def _mlp(xp, xs, g, w1, w2, gf, casts, *, tile, final_norm):
    _, Lp, D = xp.shape
    _, Ls, _ = xs.shape
    ntp = Lp // tile
    assert Lp % tile == 0
    tok = pl.BlockSpec((1, tile, D), lambda s, t: (0, jnp.minimum(t, ntp - 1), 0))
    smp = pl.BlockSpec((1, Ls, D), lambda s, t: (0, 0, 0))
    cs = [_cast_specs(w, layer, ntp) for w, layer in casts]
    outs = pl.pallas_call(
        functools.partial(_mlp_kernel, final_norm=final_norm, n_cast=len(casts)),
        grid=(1, ntp + 1),
        in_specs=[tok, smp, _whole((1, D)), _whole((D, D_FF)), _whole((D_FF, D)), _whole((1, D))]
        + [c[0] for c in cs],
        out_specs=[tok, smp] + [c[1] for c in cs],
        out_shape=[jax.ShapeDtypeStruct((1, Lp, D), F32), jax.ShapeDtypeStruct((1, Ls, D), F32)]
        + [c[2] for c in cs],
        compiler_params=_params(),
        name="mlp",
    )(xp, xs, g, w1, w2, gf, *[w for w, _ in casts])
    return outs[0], outs[1], list(outs[2:])


def _pool_group(u, pos, ext_ref, w_ref, gi, tile):
    w = POOL_WINDOWS[gi]
    c0, c1 = gi * POOL_GC, (gi + 1) * POOL_GC
    win = u[:, c0:c1]
    for j in range(1, w):
        win = win + ext_ref[POOL_HALO - j:POOL_HALO - j + tile, c0:c1]
    cnt = jnp.minimum(pos + 1, w).astype(F32)
    diff = win / cnt - u[:, c0:c1]
    return _dot(diff.astype(BF16), w_ref[gi])


def _pool_apply(x, pos0, ext_ref, g, w_ref, sc, tile):
    u = _rms(x, g)
    ext_ref[POOL_HALO:POOL_HALO + tile, :] = u
    pos = pos0 + lax.broadcasted_iota(jnp.int32, (tile, 1), 0)
    ys = [_pool_group(u, pos, ext_ref, w_ref, gi, tile) for gi in range(POOL_GROUPS)]
    y = jnp.concatenate(ys, axis=1) * sc
    return x + y, ext_ref[tile:tile + POOL_HALO, :]


def _pool_kernel(x_ref, hist_ref, g_ref, w_ref, sc_ref, o_ref, cache_ref, ext_ref, *, tile, p0):
    t = pl.program_id(1)

    @pl.when(t == 0)
    def _():
        ext_ref[0:POOL_HALO, :] = hist_ref[0]

    out, tail = _pool_apply(x_ref[0], p0 + t * tile, ext_ref, g_ref[...], w_ref, sc_ref[...], tile)
    o_ref[0] = out
    cache_ref[0] = tail
    ext_ref[0:POOL_HALO, :] = tail


def _pool_mlp_kernel(x0_ref, xn_ref, xs_ref, hist_ref, gm_ref, pw_ref, sc_ref, gffn_ref, w1_ref, w2_ref, gf_ref,
                     *rest, tile, p0, final_norm, n_cast):
    o_ref, cache_ref, os_ref = rest[n_cast:n_cast + 3]
    ext_ref, x1_ref = rest[2 * n_cast + 3:]
    _cast_chunks(rest[:n_cast], rest[n_cast + 3:2 * n_cast + 3])
    t = pl.program_id(1)
    nt = pl.num_programs(1) - 1

    @pl.when(t == nt)
    def _():
        os_ref[0] = _mlp_apply(xs_ref[0], gffn_ref[...], w1_ref, w2_ref, gf_ref[...], final_norm)

    @pl.when(t == 0)
    def _():
        ext_ref[0:POOL_HALO, :] = hist_ref[0]
        out, tail = _pool_apply(x0_ref[0], p0, ext_ref, gm_ref[...], pw_ref, sc_ref[...], tile)
        x1_ref[0] = out
        ext_ref[0:POOL_HALO, :] = tail

    @pl.when(t < nt)
    def _():
        x1 = x1_ref[lax.rem(t, 2)]
        v, r = _rms_split(x1, gffn_ref[...])
        halo = ext_ref[0:POOL_HALO, :]
        xn = xn_ref[0]
        u = _rms(xn, gm_ref[...])
        ext_ref[POOL_HALO:POOL_HALO + tile, :] = u
        pos = p0 + (t + 1) * tile + lax.broadcasted_iota(jnp.int32, (tile, 1), 0)
        fc = D_FF // POOL_GROUPS
        acc = None
        ys = []
        for c in range(POOL_GROUPS):
            h = _dot(v, w1_ref[:, c * fc:(c + 1) * fc])
            h = jnp.square(jnp.maximum(h, 0.0)).astype(BF16)
            part = _dot(h, w2_ref[c * fc:(c + 1) * fc, :])
            acc = part if acc is None else acc + part
            ys.append(_pool_group(u, pos, ext_ref, pw_ref, c, tile))
        acc = x1 + (r * r) * acc
        if final_norm:
            acc = _rms(acc, gf_ref[...])
        o_ref[0] = acc

        x1_ref[lax.rem(t + 1, 2)] = xn + jnp.concatenate(ys, axis=1) * sc_ref[...]
        halo = jnp.where(t + 1 < nt, ext_ref[tile:tile + POOL_HALO, :], halo)
        ext_ref[0:POOL_HALO, :] = halo
        cache_ref[0] = halo


def _pool_mlp(x, xs, hist, gm, pw, sc, gffn, w1, w2, gf, casts, *, pool_layer, tile, p0, final_norm):
    S, L, D = x.shape
    _, Ls, _ = xs.shape
    nt = L // tile
    assert S == 1 and L % tile == 0 and tile >= POOL_HALO
    first = pl.BlockSpec((1, tile, D), lambda s, t: (s, 0, 0))
    nxt = pl.BlockSpec((1, tile, D), lambda s, t: (s, jnp.minimum(t + 1, nt - 1), 0))
    tok = pl.BlockSpec((1, tile, D), lambda s, t: (s, jnp.minimum(t, nt - 1), 0))
    smp = pl.BlockSpec((1, Ls, D), lambda s, t: (0, 0, 0))
    seq = pl.BlockSpec((1, POOL_HALO, D), lambda s, t: (s, 0, 0))
    cs = [_cast_specs(w, layer, nt) for w, layer in casts]
    outs = pl.pallas_call(
        functools.partial(_pool_mlp_kernel, tile=tile, p0=p0, final_norm=final_norm, n_cast=len(casts)),
        grid=(S, nt + 1),
        in_specs=[first, nxt, smp, seq, _whole((1, D)), _layer((POOL_GROUPS, POOL_GC, POOL_GC), pool_layer),
                  _whole((1, D)), _whole((1, D)),
                  _whole((D, D_FF), single_buffer=True), _whole((D_FF, D), single_buffer=True), _whole((1, D))]
        + [c[0] for c in cs],
        out_specs=[tok, seq, smp] + [c[1] for c in cs],
        out_shape=[jax.ShapeDtypeStruct((S, L, D), F32), jax.ShapeDtypeStruct((S, POOL_HALO, D), F32),
                   jax.ShapeDtypeStruct((1, Ls, D), F32)] + [c[2] for c in cs],
        scratch_shapes=[pltpu.VMEM((POOL_HALO + tile, D), F32), pltpu.VMEM((2, tile, D), F32)],
        compiler_params=_params(),
        name="pool_mlp",
    )(x, x, xs, hist, gm, pw, sc, gffn, w1, w2, gf, *[w for w, _ in casts])
    return outs[0], outs[1][:, 1:, :], outs[2], list(outs[3:])


def _pool(x, hist, g, w, sc, *, pool_layer, tile, p0):
    S, L, D = x.shape
    assert L % tile == 0 and tile >= POOL_HALO
    tok = pl.BlockSpec((1, tile, D), lambda s, t: (s, t, 0))
    seq = pl.BlockSpec((1, POOL_HALO, D), lambda s, t: (s, 0, 0))
    out, cache = pl.pallas_call(
        functools.partial(_pool_kernel, tile=tile, p0=p0),
        grid=(S, L // tile),
        in_specs=[tok, seq, _whole((1, D)), _layer((POOL_GROUPS, POOL_GC, POOL_GC), pool_layer), _whole((1, D))],
        out_specs=[tok, seq],
        out_shape=[jax.ShapeDtypeStruct((S, L, D), F32), jax.ShapeDtypeStruct((S, POOL_HALO, D), F32)],
        scratch_shapes=[pltpu.VMEM((POOL_HALO + tile, D), F32)],
        compiler_params=_params(),
        name="pool",
    )(x, hist, g, w, sc)
    return out, cache[:, 1:, :]


def _build_attn_bias(rb_ref, bias_ref, *, tile):
    nb = ATTN_WINDOW // tile + 1
    w2 = 2 * tile
    col = lax.broadcasted_iota(jnp.int32, (REL_SIZE, nb * w2), 1)
    row = lax.broadcasted_iota(jnp.int32, (REL_SIZE, nb * w2), 0)
    ip = col % w2
    i = jnp.where(ip < tile, ip, ip - w2)
    idx = jnp.clip((col // w2) * tile - i, REL_MIN, REL_MAX) - REL_MIN
    sel = jnp.where(row == idx, 1.0, 0.0).astype(BF16)
    hi, mid, lo = _split3(rb_ref[...])
    gen = _dot(hi, sel) + _dot(mid, sel) + _dot(lo, sel)

    r = lax.broadcasted_iota(jnp.int32, (tile, w2), 0)
    lane = lax.broadcasted_iota(jnp.int32, (tile, w2), 1)
    first = lane < tile
    for d in range(nb):
        dchunk = (d * tile) // CHUNK + r // CHUNK - (lane % tile) // CHUNK
        ok = (dchunk >= 0) & (dchunk <= N_PAST_CHUNKS)
        for p in range(HEAD_PAIRS):
            ga = jnp.broadcast_to(gen[2 * p:2 * p + 1, d * w2:(d + 1) * w2], (tile, w2))
            gb = jnp.broadcast_to(gen[2 * p + 1:2 * p + 2, d * w2:(d + 1) * w2], (tile, w2))
            ta = pltpu.roll(ga, 0, 1, stride=1, stride_axis=0)
            tb = pltpu.roll(gb, tile, 1, stride=1, stride_axis=0)
            bias_ref[p, d] = jnp.where(ok, jnp.where(first, ta, tb) * LOG2E, NEG_INF)


def _attn_kernel(x_ref, hk_ref, hv_ref, g_ref, wqkv_ref, wo_ref, rb_ref,
                 o_ref, ko_ref, vo_ref, kt_ring, v_ring, bias_ref, *, tile, nsub, mask_start):
    D = D_MODEL
    nb = ATTN_WINDOW // tile + 1
    ring = ATTN_WINDOW // tile + nsub
    t = pl.program_id(1)

    row_lo = lax.broadcasted_iota(jnp.int32, (PAIR_W, tile), 0) < ATTN_HD
    lane_lo = lax.broadcasted_iota(jnp.int32, (tile, PAIR_W), 1) < ATTN_HD
    first = lax.broadcasted_iota(jnp.int32, (tile, 2 * tile), 1) < tile

    @pl.when((pl.program_id(0) == 0) & (t == 0))
    def _():
        _build_attn_bias(rb_ref, bias_ref, tile=tile)

    rr = lax.broadcasted_iota(jnp.int32, (2 * tile, PAIR_W), 0)
    cc = lax.broadcasted_iota(jnp.int32, (2 * tile, PAIR_W), 1)
    sum_cols = jnp.where(((cc == 0) & (rr < tile)) | ((cc == 1) & (rr >= tile)), 1.0, 0.0).astype(BF16)

    def store_block(slot, k, v):
        kt = k.T.astype(BF16)
        vb = v.astype(BF16)
        zk = jnp.zeros((PAIR_W, tile), BF16)
        zv = jnp.zeros((tile, PAIR_W), BF16)
        for p in range(HEAD_PAIRS):
            ktp = kt[p * PAIR_W:(p + 1) * PAIR_W, :]
            kt_ring[slot, p] = jnp.concatenate(
                [jnp.where(row_lo, ktp, zk), jnp.where(row_lo, zk, ktp)], axis=1)
            vp = vb[:, p * PAIR_W:(p + 1) * PAIR_W]
            v_ring[slot, p, :, 0:PAIR_W] = jnp.concatenate(
                [jnp.where(lane_lo, vp, zv), jnp.where(lane_lo, zv, vp)], axis=0)
            v_ring[slot, p, :, PAIR_W:2 * PAIR_W] = sum_cols

    @pl.when(t == 0)
    def _():
        for i in range(nb - 1):
            store_block(i + nsub, hk_ref[0, i * tile:(i + 1) * tile, :], hv_ref[0, i * tile:(i + 1) * tile, :])

    x = x_ref[0]
    xg, r = _rms_split(x, g_ref[...])
    qkv = _dot(xg, wqkv_ref[...])
    q = (qkv[:, :D] * (r * (LOG2E * ATTN_HD ** -0.5))).astype(BF16)
    k = qkv[:, D:2 * D] * r
    v = qkv[:, 2 * D:] * r
    ko_ref[0] = k
    vo_ref[0] = v
    for j in range(nsub):
        store_block(lax.rem(t * nsub + j, ring), k[j * tile:(j + 1) * tile, :], v[j * tile:(j + 1) * tile, :])

    def slot_of(j, d):
        return lax.rem(t * nsub + j + ring - d, ring)

    def score_block(j, p, d):
        s = _dot(q[j * tile:(j + 1) * tile, p * PAIR_W:(p + 1) * PAIR_W], kt_ring[slot_of(j, d), p]) + bias_ref[p, d]
        if mask_start and d > j:
            s = s + jnp.where(t * nsub + j >= d, 0.0, NEG_INF)
        return s

    def row_max(scores):
        mx = functools.reduce(jnp.maximum, scores)
        m_a = jnp.max(mx[:, :tile], axis=-1, keepdims=True)
        m_b = jnp.max(mx[:, tile:], axis=-1, keepdims=True)
        return jnp.where(first, m_a, m_b)

    units = [(j, p) for j in range(nsub) for p in range(HEAD_PAIRS)]
    outs = [[] for _ in range(nsub)]
    nxt = [score_block(0, 0, d) for d in range(nb)]
    for n, (j, p) in enumerate(units):
        scores, m = nxt, row_max(nxt)
        if n + 1 < len(units):
            nxt = [score_block(units[n + 1][0], units[n + 1][1], d) for d in range(nb)]
        acc = jnp.zeros((tile, 2 * PAIR_W), F32)
        for d in range(nb):
            pr = jnp.exp2(scores[d] - m)
            acc = acc + _dot(pr.astype(BF16), v_ring[slot_of(j, d), p])
        l_a = acc[:, PAIR_W:PAIR_W + 1]
        l_b = acc[:, PAIR_W + 1:PAIR_W + 2]
        inv = jnp.where(lane_lo, 1.0 / l_a, 1.0 / l_b)
        outs[j].append((acc[:, :PAIR_W] * inv).astype(BF16))
    o = jnp.concatenate([jnp.concatenate(row, axis=1) for row in outs], axis=0)
    o_ref[0] = x + _dot(o, wo_ref[...])


def _attn(x, hist_k, hist_v, g, wqkv, wo, rel_bias, *, tile, nsub, keep, mask_start):
    S, L, D = x.shape
    step = tile * nsub
    nb = ATTN_WINDOW // tile + 1
    nt = L // step
    nkeep = keep // step
    assert L % step == 0 and tile % CHUNK == 0 and ATTN_WINDOW % tile == 0 and keep % step == 0
    tok = pl.BlockSpec((1, step, D), lambda s, t: (s, t, 0))
    hist = pl.BlockSpec((1, ATTN_WINDOW, D), lambda s, t: (s, 0, 0))
    kv_out = pl.BlockSpec((1, step, D), lambda s, t: (s, jnp.maximum(t - (nt - nkeep), 0), 0))
    return pl.pallas_call(
        functools.partial(_attn_kernel, tile=tile, nsub=nsub, mask_start=mask_start),
        grid=(S, nt),
        in_specs=[tok, hist, hist, _whole((1, D)), _whole((D, 3 * D)), _whole((D, D)),
                  _whole((ATTN_HEADS, REL_SIZE))],
        out_specs=[tok, kv_out, kv_out],
        out_shape=[jax.ShapeDtypeStruct((S, L, D), F32),
                   jax.ShapeDtypeStruct((S, keep, D), F32),
                   jax.ShapeDtypeStruct((S, keep, D), F32)],
        scratch_shapes=[pltpu.VMEM((nb - 1 + nsub, HEAD_PAIRS, PAIR_W, 2 * tile), BF16),
                        pltpu.VMEM((nb - 1 + nsub, HEAD_PAIRS, 2 * tile, 2 * PAIR_W), BF16),
                        pltpu.VMEM((HEAD_PAIRS, nb, tile, 2 * tile), F32)],
        compiler_params=_params(),
        name="attn",
    )(x, hist_k, hist_v, g, wqkv, wo, rel_bias)


def _tile_row(x, n, s):
    C, W = x.shape
    xr = x.reshape(C // n, n, W)
    return jnp.broadcast_to(xr[:, s:s + 1, :], (C // n, n, W)).reshape(C, W)


def _hgrn_intra_fast(q, kk, b2, vb, tril):
    C = q.shape[0]
    bmid = b2[C // 2 - 1:C // 2, :]
    qt = (q * jnp.exp2(b2 - bmid)).astype(BF16)
    kt = (kk * jnp.exp2(bmid - b2)).T.astype(BF16)
    amat = jnp.where(tril, _dot(qt, kt), 0.0)
    return _dot(amat.astype(BF16), vb)


def _hgrn_intra_safe(q, kk, b2, vv, vb, levels, rowid, ones):
    C = q.shape[0]
    amat = jnp.zeros((C, C), F32)
    for hl, upper, same in levels:
        bref = _tile_row(b2, 2 * hl, hl - 1)
        qt = q * jnp.exp2(jnp.where(upper, b2 - bref, NEG_INF))
        kt = kk * jnp.exp2(jnp.where(upper, NEG_INF, bref - b2))
        amat = amat + jnp.where(same, _dot(qt.astype(BF16), kt.T.astype(BF16)), 0.0)
    o = _dot(amat.astype(BF16), vb)
    sub = rowid % HG_DIAG
    for s in range(HG_DIAG):
        dec = jnp.exp2(jnp.where(sub >= s, b2 - _tile_row(b2, HG_DIAG, s), NEG_INF))
        ps = q * _tile_row(kk, HG_DIAG, s) * dec
        o = o + _dot(ps.astype(BF16), ones) * _tile_row(vv, HG_DIAG, s)
    return o


def _hgrn_kernel(x_ref, s0_ref, g_ref, win_ref, wo_ref, gn_ref, lbp_ref,
                 o_ref, so_ref, st_ref, pj_ref, b2_ref, og_ref, *, chunk, nsub, layer):
    D = D_MODEL
    C = chunk
    t = pl.program_id(1)
    nt = pl.num_programs(1)

    @pl.when(t == 0)
    def _():
        for h in range(HG_HEADS):
            st_ref[h] = s0_ref[0, h].T

    x = x_ref[0]
    xg, rinv = _rms_split(x, g_ref[...])
    pj_ref[...] = _dot(xg, win_ref[...]) * rinv

    lbp = lbp_ref[...]
    e = jnp.exp(lbp - jnp.max(lbp, axis=0, keepdims=True))
    sm = e / jnp.sum(e, axis=0, keepdims=True)
    lb = jnp.zeros((1, D), F32)
    for i in range(1, layer + 1):
        lb = lb + sm[i:i + 1, :]
    log_lb = jnp.log(lb)
    log_1mlb = jnp.log1p(-lb)

    zf = pj_ref[:, D:2 * D]
    a = jnp.exp(-jnp.abs(zf))
    r = 1.0 / (1.0 + a)
    log_sig = jnp.minimum(zf, 0.0) - jnp.log(1.0 + a)
    bb = log_1mlb + log_sig
    logf = jnp.maximum(log_lb, bb) + jnp.log(1.0 + jnp.exp(-jnp.abs(log_lb - bb)))
    pj_ref[:, D:2 * D] = (1.0 - lb) * jnp.where(zf >= 0.0, a * r, r)
    zq = pj_ref[:, :D]
    pj_ref[:, :D] = zq * (1.0 / (1.0 + jnp.exp(-zq)))

    ri = lax.broadcasted_iota(jnp.int32, (C, C), 0)
    ci = lax.broadcasted_iota(jnp.int32, (C, C), 1)
    tril = ri >= ci
    tri = tril.astype(BF16)
    ends = []
    safe = None
    for c in range(nsub):
        hi, mid, lo = _split3(logf[c * C:(c + 1) * C, :])
        b2_c = (_dot(tri, hi) + _dot(tri, mid) + _dot(tri, lo)) * LOG2E
        b2_ref[c * C:(c + 1) * C, :] = b2_c
        b2_mid = b2_c[C // 2 - 1:C // 2, :]
        b2_end = b2_c[C - 1:C, :]
        ends.append(b2_end)
        ok = (jnp.min(b2_mid) >= -HG_SAFE_LOG2) & (jnp.min(b2_end - b2_mid) >= -HG_SAFE_LOG2)
        safe = ok if safe is None else safe & ok

    def finish(intra):
        for c in range(nsub):
            rows = slice(c * C, (c + 1) * C)
            for h in range(HG_HEADS):
                cols = slice(h * HG_DK, (h + 1) * HG_DK)
                q, kk, b2 = pj_ref[rows, cols], pj_ref[rows, D + h * HG_DK:D + (h + 1) * HG_DK], b2_ref[rows, cols]
                vv = pj_ref[rows, 2 * D + h * HG_DV:2 * D + (h + 1) * HG_DV]
                zg = pj_ref[rows, 3 * D + h * HG_DV:3 * D + (h + 1) * HG_DV]
                vb = vv.astype(BF16)
                st = st_ref[h]
                end = ends[c][:, cols]
                o = _dot((q * jnp.exp2(b2)).astype(BF16), st.T.astype(BF16))
                kdec = (kk * jnp.exp2(end - b2)).astype(BF16)
                st_ref[h] = st * jnp.exp2(end) + _dot_tn(vb, kdec)
                o = o + intra(q, kk, b2, vv, vb)
                on = o * lax.rsqrt(jnp.mean(o * o, axis=-1, keepdims=True) + EPS) * gn_ref[...]
                og_ref[rows, h * HG_DV:(h + 1) * HG_DV] = (on * (zg * (1.0 / (1.0 + jnp.exp(-zg))))).astype(BF16)
        o_ref[0] = x + _dot(og_ref[...], wo_ref[...])

    @pl.when(safe)
    def _():
        finish(lambda q, kk, b2, vv, vb: _hgrn_intra_fast(q, kk, b2, vb, tril))

    @pl.when(jnp.logical_not(safe))
    def _():
        rowid = lax.broadcasted_iota(jnp.int32, (C, 1), 0)
        ones = jnp.ones((HG_DK, HG_DV), BF16)
        levels = []
        hl = C // 2
        while hl >= HG_DIAG:
            levels.append((hl, (rowid % (2 * hl)) >= hl, (ri // (2 * hl)) == (ci // (2 * hl))))
            hl //= 2
        finish(lambda q, kk, b2, vv, vb: _hgrn_intra_safe(q, kk, b2, vv, vb, levels, rowid, ones))

    @pl.when(t == nt - 1)
    def _():
        for h in range(HG_HEADS):
            so_ref[0, h] = st_ref[h].T


def _hgrn(x, s0, g, w_in, w_o, gn, lbp, *, chunk, nsub, layer):
    S, L, D = x.shape
    tile = chunk * nsub
    assert L % tile == 0 and chunk % (2 * HG_DIAG) == 0
    tok = pl.BlockSpec((1, tile, D), lambda s, t: (s, t, 0))
    state = pl.BlockSpec((1, HG_HEADS, HG_DK, HG_DV), lambda s, t: (s, 0, 0, 0))
    return pl.pallas_call(
        functools.partial(_hgrn_kernel, chunk=chunk, nsub=nsub, layer=layer),
        grid=(S, L // tile),
        in_specs=[tok, state, _whole((1, D)), _whole((D, 4 * D)), _whole((D, D)), _whole((1, HG_DV)),
                  _whole((DEPTH, D))],
        out_specs=[tok, state],
        out_shape=[jax.ShapeDtypeStruct((S, L, D), F32),
                   jax.ShapeDtypeStruct((S, HG_HEADS, HG_DK, HG_DV), F32)],
        scratch_shapes=[pltpu.VMEM((HG_HEADS, HG_DV, HG_DK), F32),
                        pltpu.VMEM((tile, 4 * D), F32),
                        pltpu.VMEM((tile, D), F32),
                        pltpu.VMEM((tile, D), BF16)],
        compiler_params=_params(),
        name="hgrn",
    )(x, s0, g, w_in, w_o, gn, lbp)


MLP_TILE = 512
POOL_TILE = (512, 64)
ATTN_TILE = (128, 64)
ATTN_NSUB = (2, 1)
HGRN_CHUNK = (128, 64)
HGRN_NSUB = (4, 1)


def kernel(x_prompt, x_sample, cache_pool, cache_attn_k, cache_attn_v, state_hgrn, norm_mix, norm_ffn,
           norm_final, pool_w, pool_scale, attn_wqkv, attn_wo, attn_rel_bias, hgrn_w_in, hgrn_w_o,
           hgrn_gnorm, hgrn_lower_bounds, mlp_w1, mlp_w2):
    B, L, D = x_prompt.shape
    BS, LS, _ = x_sample.shape
    xp = x_prompt
    xs = x_sample
    gf = norm_final.reshape(1, D)
    def next_casts(i):
        if i + 1 >= DEPTH:
            return []
        kind, j = (i + 1) % N_MIXERS, (i + 1) // N_MIXERS
        casts = [(mlp_w1, i + 1), (mlp_w2, i + 1)]
        if kind == 1:
            casts += [(attn_wqkv, j), (attn_wo, j)]
        elif kind == 2:
            casts += [(hgrn_w_in, j), (hgrn_w_o, j)]
        return casts

    w1 = mlp_w1[0].astype(BF16)
    w2 = mlp_w2[0].astype(BF16)
    mix_w = []
    pw_all = pool_w.astype(BF16)

    pool_p, pool_s, k_p, k_s, v_p, v_s, h_p, h_s = [], [], [], [], [], [], [], []
    for i in range(DEPTH):
        kind, j = i % N_MIXERS, i // N_MIXERS
        g = norm_mix[i].reshape(1, D)
        gm = norm_ffn[i].reshape(1, D)
        last = i == DEPTH - 1
        if kind == 0:
            sc = pool_scale[j].reshape(1, D)
            hist_p = jnp.zeros((B, POOL_HALO, D), F32)
            hist_s = jnp.pad(cache_pool[j], ((0, 0), (1, 0), (0, 0)))
            xs, cs = _pool(xs, hist_s, g, pw_all, sc, pool_layer=j, tile=POOL_TILE[1], p0=PAST_LEN)
            xp, cp, xs, cast = _pool_mlp(xp, xs.reshape(1, BS * LS, D), hist_p, g, pw_all, sc, gm, w1, w2, gf,
                                         next_casts(i), pool_layer=j, tile=MLP_TILE, p0=0, final_norm=last)
            pool_p.append(cp)
            pool_s.append(cs)
        else:
            if kind == 1:
                wqkv, wo = mix_w
                keep_p = min(ATTN_WINDOW, L)
                zeros = jnp.zeros((B, ATTN_WINDOW, D), F32)
                xp, kp, vp = _attn(xp, zeros, zeros, g, wqkv, wo, attn_rel_bias[j],
                                   tile=ATTN_TILE[0], nsub=ATTN_NSUB[0], keep=keep_p, mask_start=True)
                ck = cache_attn_k[j].reshape(BS, ATTN_WINDOW, D)
                cv = cache_attn_v[j].reshape(BS, ATTN_WINDOW, D)
                xs, ks, vs = _attn(xs, ck, cv, g, wqkv, wo, attn_rel_bias[j],
                                   tile=ATTN_TILE[1], nsub=ATTN_NSUB[1], keep=LS, mask_start=False)
                k_p.append(kp.reshape(B, keep_p, ATTN_HEADS, ATTN_HD))
                v_p.append(vp.reshape(B, keep_p, ATTN_HEADS, ATTN_HD))
                k_s.append(ks.reshape(BS, LS, ATTN_HEADS, ATTN_HD))
                v_s.append(vs.reshape(BS, LS, ATTN_HEADS, ATTN_HD))
            else:
                w_in, w_o = mix_w
                gn = hgrn_gnorm[j].reshape(1, HG_DV)
                s0 = jnp.zeros((B, HG_HEADS, HG_DK, HG_DV), F32)
                xp, sp = _hgrn(xp, s0, g, w_in, w_o, gn, hgrn_lower_bounds, chunk=HGRN_CHUNK[0],
                               nsub=HGRN_NSUB[0], layer=i)
                xs, ss = _hgrn(xs, state_hgrn[j], g, w_in, w_o, gn, hgrn_lower_bounds, chunk=HGRN_CHUNK[1],
                               nsub=HGRN_NSUB[1], layer=i)
                h_p.append(sp)
                h_s.append(ss)
            xp, xs, cast = _mlp(xp.reshape(1, B * L, D), xs.reshape(1, BS * LS, D), gm, w1, w2, gf,
                                next_casts(i), tile=MLP_TILE, final_norm=last)
            xp = xp.reshape(B, L, D)
        xs = xs.reshape(BS, LS, D)
        if cast:
            w1, w2, mix_w = cast[0], cast[1], cast[2:]

    return (xp, xs, jnp.stack(pool_p), jnp.stack(pool_s), jnp.stack(k_p), jnp.stack(k_s),
            jnp.stack(v_p), jnp.stack(v_s), jnp.stack(h_p), jnp.stack(h_s))
```

```python
import functools
import math

import jax
import jax.numpy as jnp
from jax import lax
from jax.experimental import pallas as pl
from jax.experimental.pallas import tpu as pltpu

F32 = jnp.float32
BF16 = jnp.bfloat16

D_MODEL = 1024
DEPTH = 4
CHUNK = 64
N_MIXERS = 3
PAST_LEN = 2048
EPS = 1e-6
POOL_WINDOWS = (2, 4, 8, 16)
POOL_GROUPS = 4
POOL_GC = D_MODEL // POOL_GROUPS
POOL_HIST = max(POOL_WINDOWS) - 1
POOL_HALO = POOL_HIST + 1
ATTN_HEADS = 16
ATTN_HD = D_MODEL // ATTN_HEADS
N_PAST_CHUNKS = 8
ATTN_WINDOW = N_PAST_CHUNKS * CHUNK
REL_MIN = -(CHUNK - 1)
REL_MAX = 256
REL_SIZE = REL_MAX - REL_MIN + 1
HEAD_PAIRS = ATTN_HEADS // 2
PAIR_W = 2 * ATTN_HD
HG_HEADS = 8
HG_DK = D_MODEL // HG_HEADS
HG_DV = D_MODEL // HG_HEADS
HG_DIAG = 8
HG_SAFE_LOG2 = 100.0
LOG2E = math.log2(math.e)
D_FF = 4 * D_MODEL

V7X_VMEM_BYTES = 64 * 1024 * 1024
VMEM_LIMIT_BYTES = V7X_VMEM_BYTES - 8 * 1024 * 1024

NEG_INF = float("-inf")


def _params():
    return pltpu.CompilerParams(dimension_semantics=("arbitrary", "arbitrary"),
                                vmem_limit_bytes=VMEM_LIMIT_BYTES)


def _whole(shape, single_buffer=False):
    zeros = (0,) * len(shape)
    kw = dict(pipeline_mode=pl.Buffered(1)) if single_buffer else {}
    return pl.BlockSpec(shape, lambda s, t: zeros, **kw)


def _layer(shape, i):
    zeros = (0,) * len(shape)
    return pl.BlockSpec((None,) + tuple(shape), lambda s, t: (i,) + zeros)


def _cast_specs(w, layer, nchunk, axis=1):
    _, rows, cols = w.shape
    assert rows % nchunk == 0 and (rows // nchunk) % 16 == 0
    cr = rows // nchunk
    chunk = lambda s, t: jnp.minimum((s, t)[axis], nchunk - 1)
    src = pl.BlockSpec((None, cr, cols), lambda s, t: (layer, chunk(s, t), 0))
    dst = pl.BlockSpec((cr, cols), lambda s, t: (chunk(s, t), 0))
    return src, dst, jax.ShapeDtypeStruct((rows, cols), BF16)


def _cast_chunks(src_refs, dst_refs):
    for s_ref, d_ref in zip(src_refs, dst_refs):
        d_ref[...] = s_ref[...].astype(BF16)


def _rms(x, g):
    ms = jnp.mean(x * x, axis=-1, keepdims=True)
    return x * lax.rsqrt(ms + EPS) * g


def _rms_split(x, g):
    r = lax.rsqrt(jnp.mean(x * x, axis=-1, keepdims=True) + EPS)
    return (x * g).astype(BF16), r


def _dot(a, b):
    return jnp.dot(a, b, preferred_element_type=F32)


def _dot_tn(a, b):
    return lax.dot_general(a, b, (((0,), (0,)), ((), ())), preferred_element_type=F32)


def _split3(x):
    hi = x.astype(BF16)
    r1 = x - hi.astype(F32)
    mid = r1.astype(BF16)
    lo = (r1 - mid.astype(F32)).astype(BF16)
    return hi, mid, lo


def _mlp_apply(x, g, w1_ref, w2_ref, gf, final_norm):
    xg, r = _rms_split(x, g)
    h = _dot(xg, w1_ref[...])
    h = jnp.square(jnp.maximum(h, 0.0)).astype(BF16)
    y = x + (r * r) * _dot(h, w2_ref[...])
    if final_norm:
        y = _rms(y, gf)
    return y


def _mlp_kernel(xp_ref, xs_ref, g_ref, w1_ref, w2_ref, gf_ref, *rest, final_norm, n_cast):
    op_ref, os_ref = rest[n_cast:n_cast + 2]
    _cast_chunks(rest[:n_cast], rest[n_cast + 2:])
    t = pl.program_id(1)
    last = pl.num_programs(1) - 1

    @pl.when(t < last)
    def _():
        op_ref[0] = _mlp_apply(xp_ref[0], g_ref[...], w1_ref, w2_ref, gf_ref[...], final_norm)

    @pl.when(t == last)
    def _():
        os_ref[0] = _mlp_apply(xs_ref[0], g_ref[...], w1_ref, w2_ref, gf_ref[...], final_norm)


def _mlp(xp, xs, g, w1, w2, gf, casts, *, tile, final_norm):
    _, Lp, D = xp.shape
    _, Ls, _ = xs.shape
    ntp = Lp // tile
    assert Lp % tile == 0
    tok = pl.BlockSpec((1, tile, D), lambda s, t: (0, jnp.minimum(t, ntp - 1), 0))
    smp = pl.BlockSpec((1, Ls, D), lambda s, t: (0, 0, 0))
    cs = [_cast_specs(w, layer, ntp) for w, layer in casts]
    outs = pl.pallas_call(
        functools.partial(_mlp_kernel, final_norm=final_norm, n_cast=len(casts)),
        grid=(1, ntp + 1),
        in_specs=[tok, smp, _whole((1, D)), _whole((D, D_FF)), _whole((D_FF, D)), _whole((1, D))]
        + [c[0] for c in cs],
        out_specs=[tok, smp] + [c[1] for c in cs],
        out_shape=[jax.ShapeDtypeStruct((1, Lp, D), F32), jax.ShapeDtypeStruct((1, Ls, D), F32)]
        + [c[2] for c in cs],
        compiler_params=_params(),
        name="mlp",
    )(xp, xs, g, w1, w2, gf, *[w for w, _ in casts])
    return outs[0], outs[1], list(outs[2:])


def _pool_group(u, pos, ext_ref, w_ref, gi, tile):
    w = POOL_WINDOWS[gi]
    c0, c1 = gi * POOL_GC, (gi + 1) * POOL_GC
    win = u[:, c0:c1]
    for j in range(1, w):
        win = win + ext_ref[POOL_HALO - j:POOL_HALO - j + tile, c0:c1]
    cnt = jnp.minimum(pos + 1, w).astype(F32)
    diff = win / cnt - u[:, c0:c1]
    return _dot(diff.astype(BF16), w_ref[gi])


def _pool_apply(x, pos0, ext_ref, g, w_ref, sc, tile):
    u = _rms(x, g)
    ext_ref[POOL_HALO:POOL_HALO + tile, :] = u
    pos = pos0 + lax.broadcasted_iota(jnp.int32, (tile, 1), 0)
    ys = [_pool_group(u, pos, ext_ref, w_ref, gi, tile) for gi in range(POOL_GROUPS)]
    y = jnp.concatenate(ys, axis=1) * sc
    return x + y, ext_ref[tile:tile + POOL_HALO, :]


def _pool_kernel(x_ref, hist_ref, g_ref, w_ref, sc_ref, *rest, tile, p0, n_cast):
    o_ref, cache_ref = rest[n_cast:n_cast + 2]
    ext_ref = rest[2 * n_cast + 2]
    _cast_chunks(rest[:n_cast], rest[n_cast + 2:2 * n_cast + 2])
    t = pl.program_id(1)

    @pl.when(t == 0)
    def _():
        ext_ref[0:POOL_HALO, :] = hist_ref[0]

    out, tail = _pool_apply(x_ref[0], p0 + t * tile, ext_ref, g_ref[...], w_ref, sc_ref[...], tile)
    o_ref[0] = out
    cache_ref[0] = tail
    ext_ref[0:POOL_HALO, :] = tail


def _pool_mlp_kernel(x0_ref, xn_ref, xs_ref, hist_ref, gm_ref, pw_ref, sc_ref, gffn_ref, w1_ref, w2_ref, gf_ref,
                     *rest, tile, p0, final_norm, n_cast):
    o_ref, cache_ref, os_ref = rest[n_cast:n_cast + 3]
    ext_ref, x1_ref = rest[2 * n_cast + 3:]
    _cast_chunks(rest[:n_cast], rest[n_cast + 3:2 * n_cast + 3])
    t = pl.program_id(1)
    nt = pl.num_programs(1) - 1

    @pl.when(t == nt)
    def _():
        os_ref[0] = _mlp_apply(xs_ref[0], gffn_ref[...], w1_ref, w2_ref, gf_ref[...], final_norm)

    @pl.when(t == 0)
    def _():
        ext_ref[0:POOL_HALO, :] = hist_ref[0]
        out, tail = _pool_apply(x0_ref[0], p0, ext_ref, gm_ref[...], pw_ref, sc_ref[...], tile)
        x1_ref[0] = out
        ext_ref[0:POOL_HALO, :] = tail

    @pl.when(t < nt)
    def _():
        x1 = x1_ref[lax.rem(t, 2)]
        v, r = _rms_split(x1, gffn_ref[...])
        halo = ext_ref[0:POOL_HALO, :]
        xn = xn_ref[0]
        u = _rms(xn, gm_ref[...])
        ext_ref[POOL_HALO:POOL_HALO + tile, :] = u
        pos = p0 + (t + 1) * tile + lax.broadcasted_iota(jnp.int32, (tile, 1), 0)
        fc = D_FF // POOL_GROUPS
        acc = None
        ys = []
        for c in range(POOL_GROUPS):
            h = _dot(v, w1_ref[:, c * fc:(c + 1) * fc])
            h = jnp.square(jnp.maximum(h, 0.0)).astype(BF16)
            part = _dot(h, w2_ref[c * fc:(c + 1) * fc, :])
            acc = part if acc is None else acc + part
            ys.append(_pool_group(u, pos, ext_ref, pw_ref, c, tile))
        acc = x1 + (r * r) * acc
        if final_norm:
            acc = _rms(acc, gf_ref[...])
        o_ref[0] = acc

        x1_ref[lax.rem(t + 1, 2)] = xn + jnp.concatenate(ys, axis=1) * sc_ref[...]
        halo = jnp.where(t + 1 < nt, ext_ref[tile:tile + POOL_HALO, :], halo)
        ext_ref[0:POOL_HALO, :] = halo
        cache_ref[0] = halo


def _pool_mlp(x, xs, hist, gm, pw, sc, gffn, w1, w2, gf, casts, *, pool_layer, tile, p0, final_norm):
    S, L, D = x.shape
    _, Ls, _ = xs.shape
    nt = L // tile
    assert S == 1 and L % tile == 0 and tile >= POOL_HALO
    first = pl.BlockSpec((1, tile, D), lambda s, t: (s, 0, 0))
    nxt = pl.BlockSpec((1, tile, D), lambda s, t: (s, jnp.minimum(t + 1, nt - 1), 0))
    tok = pl.BlockSpec((1, tile, D), lambda s, t: (s, jnp.minimum(t, nt - 1), 0))
    smp = pl.BlockSpec((1, Ls, D), lambda s, t: (0, 0, 0))
    seq = pl.BlockSpec((1, POOL_HALO, D), lambda s, t: (s, 0, 0))
    cs = [_cast_specs(w, layer, nt) for w, layer in casts]
    outs = pl.pallas_call(
        functools.partial(_pool_mlp_kernel, tile=tile, p0=p0, final_norm=final_norm, n_cast=len(casts)),
        grid=(S, nt + 1),
        in_specs=[first, nxt, smp, seq, _whole((1, D)), _layer((POOL_GROUPS, POOL_GC, POOL_GC), pool_layer),
                  _whole((1, D)), _whole((1, D)),
                  _whole((D, D_FF), single_buffer=True), _whole((D_FF, D), single_buffer=True), _whole((1, D))]
        + [c[0] for c in cs],
        out_specs=[tok, seq, smp] + [c[1] for c in cs],
        out_shape=[jax.ShapeDtypeStruct((S, L, D), F32), jax.ShapeDtypeStruct((S, POOL_HALO, D), F32),
                   jax.ShapeDtypeStruct((1, Ls, D), F32)] + [c[2] for c in cs],
        scratch_shapes=[pltpu.VMEM((POOL_HALO + tile, D), F32), pltpu.VMEM((2, tile, D), F32)],
        compiler_params=_params(),
        name="pool_mlp",
    )(x, x, xs, hist, gm, pw, sc, gffn, w1, w2, gf, *[w for w, _ in casts])
    return outs[0], outs[1][:, 1:, :], outs[2], list(outs[3:])


def _pool(x, hist, g, w, sc, casts, *, pool_layer, tile, p0):
    S, L, D = x.shape
    assert L % tile == 0 and tile >= POOL_HALO and (not casts or L == tile)
    tok = pl.BlockSpec((1, tile, D), lambda s, t: (s, t, 0))
    seq = pl.BlockSpec((1, POOL_HALO, D), lambda s, t: (s, 0, 0))
    cs = [_cast_specs(wc, layer, S, axis=0) for wc, layer in casts]
    outs = pl.pallas_call(
        functools.partial(_pool_kernel, tile=tile, p0=p0, n_cast=len(casts)),
        grid=(S, L // tile),
        in_specs=[tok, seq, _whole((1, D)), _layer((POOL_GROUPS, POOL_GC, POOL_GC), pool_layer), _whole((1, D))]
        + [c[0] for c in cs],
        out_specs=[tok, seq] + [c[1] for c in cs],
        out_shape=[jax.ShapeDtypeStruct((S, L, D), F32), jax.ShapeDtypeStruct((S, POOL_HALO, D), F32)]
        + [c[2] for c in cs],
        scratch_shapes=[pltpu.VMEM((POOL_HALO + tile, D), F32)],
        compiler_params=_params(),
        name="pool",
    )(x, hist, g, w, sc, *[wc for wc, _ in casts])
    return outs[0], outs[1][:, 1:, :], list(outs[2:])


def _build_attn_bias(rb_ref, bias_ref, *, tile):
    nb = ATTN_WINDOW // tile + 1
    w2 = 2 * tile
    col = lax.broadcasted_iota(jnp.int32, (REL_SIZE, nb * w2), 1)
    row = lax.broadcasted_iota(jnp.int32, (REL_SIZE, nb * w2), 0)
    ip = col % w2
    i = jnp.where(ip < tile, ip, ip - w2)
    idx = jnp.clip((col // w2) * tile - i, REL_MIN, REL_MAX) - REL_MIN
    sel = jnp.where(row == idx, 1.0, 0.0).astype(BF16)
    hi, mid, lo = _split3(rb_ref[...])
    gen = _dot(hi, sel) + _dot(mid, sel) + _dot(lo, sel)

    r = lax.broadcasted_iota(jnp.int32, (tile, w2), 0)
    lane = lax.broadcasted_iota(jnp.int32, (tile, w2), 1)
    first = lane < tile
    for d in range(nb):
        dchunk = (d * tile) // CHUNK + r // CHUNK - (lane % tile) // CHUNK
        ok = (dchunk >= 0) & (dchunk <= N_PAST_CHUNKS)
        for p in range(HEAD_PAIRS):
            ga = jnp.broadcast_to(gen[2 * p:2 * p + 1, d * w2:(d + 1) * w2], (tile, w2))
            gb = jnp.broadcast_to(gen[2 * p + 1:2 * p + 2, d * w2:(d + 1) * w2], (tile, w2))
            ta = pltpu.roll(ga, 0, 1, stride=1, stride_axis=0)
            tb = pltpu.roll(gb, tile, 1, stride=1, stride_axis=0)
            bias_ref[p, d] = jnp.where(ok, jnp.where(first, ta, tb) * LOG2E, NEG_INF)


def _attn_kernel(x_ref, hk_ref, hv_ref, g_ref, wqkv_ref, wo_ref, rb_ref,
                 o_ref, ko_ref, vo_ref, kt_ring, v_ring, bias_ref, *, tile, nsub, mask_start):
    D = D_MODEL
    nb = ATTN_WINDOW // tile + 1
    ring = ATTN_WINDOW // tile + nsub
    t = pl.program_id(1)

    row_lo = lax.broadcasted_iota(jnp.int32, (PAIR_W, tile), 0) < ATTN_HD
    lane_lo = lax.broadcasted_iota(jnp.int32, (tile, PAIR_W), 1) < ATTN_HD
    first = lax.broadcasted_iota(jnp.int32, (tile, 2 * tile), 1) < tile

    @pl.when((pl.program_id(0) == 0) & (t == 0))
    def _():
        _build_attn_bias(rb_ref, bias_ref, tile=tile)

    rr = lax.broadcasted_iota(jnp.int32, (2 * tile, PAIR_W), 0)
    cc = lax.broadcasted_iota(jnp.int32, (2 * tile, PAIR_W), 1)
    sum_cols = jnp.where(((cc == 0) & (rr < tile)) | ((cc == 1) & (rr >= tile)), 1.0, 0.0).astype(BF16)

    def store_block(slot, k, v):
        kt = k.T.astype(BF16)
        vb = v.astype(BF16)
        zk = jnp.zeros((PAIR_W, tile), BF16)
        zv = jnp.zeros((tile, PAIR_W), BF16)
        for p in range(HEAD_PAIRS):
            ktp = kt[p * PAIR_W:(p + 1) * PAIR_W, :]
            kt_ring[slot, p] = jnp.concatenate(
                [jnp.where(row_lo, ktp, zk), jnp.where(row_lo, zk, ktp)], axis=1)
            vp = vb[:, p * PAIR_W:(p + 1) * PAIR_W]
            v_ring[slot, p, :, 0:PAIR_W] = jnp.concatenate(
                [jnp.where(lane_lo, vp, zv), jnp.where(lane_lo, zv, vp)], axis=0)
            v_ring[slot, p, :, PAIR_W:2 * PAIR_W] = sum_cols

    def heads_to_rows(ref, i):
        x4 = jnp.swapaxes(ref[0, i * tile:(i + 1) * tile], 0, 1)
        return jnp.concatenate([x4[h] for h in range(ATTN_HEADS)], axis=1)

    @pl.when(t == 0)
    def _():
        for i in range(nb - 1):
            store_block(i + nsub, heads_to_rows(hk_ref, i), heads_to_rows(hv_ref, i))

    x = x_ref[0]
    xg, r = _rms_split(x, g_ref[...])
    qkv = _dot(xg, wqkv_ref[...])
    q = (qkv[:, :D] * (r * (LOG2E * ATTN_HD ** -0.5))).astype(BF16)
    k = qkv[:, D:2 * D] * r
    v = qkv[:, 2 * D:] * r
    ko_ref[0] = k
    vo_ref[0] = v
    for j in range(nsub):
        store_block(lax.rem(t * nsub + j, ring), k[j * tile:(j + 1) * tile, :], v[j * tile:(j + 1) * tile, :])

    def slot_of(j, d):
        return lax.rem(t * nsub + j + ring - d, ring)

    def score_block(j, p, d):
        s = _dot(q[j * tile:(j + 1) * tile, p * PAIR_W:(p + 1) * PAIR_W], kt_ring[slot_of(j, d), p]) + bias_ref[p, d]
        if mask_start and d > j:
            s = s + jnp.where(t * nsub + j >= d, 0.0, NEG_INF)
        return s

    def row_max(scores):
        mx = functools.reduce(jnp.maximum, scores)
        m_a = jnp.max(mx[:, :tile], axis=-1, keepdims=True)
        m_b = jnp.max(mx[:, tile:], axis=-1, keepdims=True)
        return jnp.where(first, m_a, m_b)

    units = [(j, p) for j in range(nsub) for p in range(HEAD_PAIRS)]
    outs = [[] for _ in range(nsub)]
    nxt = [score_block(0, 0, d) for d in range(nb)]
    for n, (j, p) in enumerate(units):
        scores, m = nxt, row_max(nxt)
        if n + 1 < len(units):
            nxt = [score_block(units[n + 1][0], units[n + 1][1], d) for d in range(nb)]
        acc = jnp.zeros((tile, 2 * PAIR_W), F32)
        for d in range(nb):
            pr = jnp.exp2(scores[d] - m)
            acc = acc + _dot(pr.astype(BF16), v_ring[slot_of(j, d), p])
        l_a = acc[:, PAIR_W:PAIR_W + 1]
        l_b = acc[:, PAIR_W + 1:PAIR_W + 2]
        inv = jnp.where(lane_lo, 1.0 / l_a, 1.0 / l_b)
        outs[j].append((acc[:, :PAIR_W] * inv).astype(BF16))
    o = jnp.concatenate([jnp.concatenate(row, axis=1) for row in outs], axis=0)
    o_ref[0] = x + _dot(o, wo_ref[...])


def _attn(x, hist_k, hist_v, g, wqkv, wo, rel_bias, *, tile, nsub, keep, mask_start):
    S, L, D = x.shape
    step = tile * nsub
    nb = ATTN_WINDOW // tile + 1
    nt = L // step
    nkeep = keep // step
    assert L % step == 0 and tile % CHUNK == 0 and ATTN_WINDOW % tile == 0 and keep % step == 0
    tok = pl.BlockSpec((1, step, D), lambda s, t: (s, t, 0))
    hist = pl.BlockSpec((1, ATTN_WINDOW, ATTN_HEADS, ATTN_HD), lambda s, t: (s, 0, 0, 0))
    kv_out = pl.BlockSpec((1, step, D), lambda s, t: (s, jnp.maximum(t - (nt - nkeep), 0), 0))
    return pl.pallas_call(
        functools.partial(_attn_kernel, tile=tile, nsub=nsub, mask_start=mask_start),
        grid=(S, nt),
        in_specs=[tok, hist, hist, _whole((1, D)), _whole((D, 3 * D)), _whole((D, D)),
                  _whole((ATTN_HEADS, REL_SIZE))],
        out_specs=[tok, kv_out, kv_out],
        out_shape=[jax.ShapeDtypeStruct((S, L, D), F32),
                   jax.ShapeDtypeStruct((S, keep, D), F32),
                   jax.ShapeDtypeStruct((S, keep, D), F32)],
        scratch_shapes=[pltpu.VMEM((nb - 1 + nsub, HEAD_PAIRS, PAIR_W, 2 * tile), BF16),
                        pltpu.VMEM((nb - 1 + nsub, HEAD_PAIRS, 2 * tile, 2 * PAIR_W), BF16),
                        pltpu.VMEM((HEAD_PAIRS, nb, tile, 2 * tile), F32)],
        compiler_params=_params(),
        name="attn",
    )(x, hist_k, hist_v, g, wqkv, wo, rel_bias)


def _tile_row(x, n, s):
    C, W = x.shape
    xr = x.reshape(C // n, n, W)
    return jnp.broadcast_to(xr[:, s:s + 1, :], (C // n, n, W)).reshape(C, W)


def _hgrn_intra_fast(q, kk, b2, vb, tril):
    C = q.shape[0]
    bmid = b2[C // 2 - 1:C // 2, :]
    qt = (q * jnp.exp2(b2 - bmid)).astype(BF16)
    kt = (kk * jnp.exp2(bmid - b2)).T.astype(BF16)
    amat = jnp.where(tril, _dot(qt, kt), 0.0)
    return _dot(amat.astype(BF16), vb)


def _hgrn_intra_safe(q, kk, b2, vv, vb, levels, rowid, ones):
    C = q.shape[0]
    amat = jnp.zeros((C, C), F32)
    for hl, upper, same in levels:
        bref = _tile_row(b2, 2 * hl, hl - 1)
        qt = q * jnp.exp2(jnp.where(upper, b2 - bref, NEG_INF))
        kt = kk * jnp.exp2(jnp.where(upper, NEG_INF, bref - b2))
        amat = amat + jnp.where(same, _dot(qt.astype(BF16), kt.T.astype(BF16)), 0.0)
    o = _dot(amat.astype(BF16), vb)
    sub = rowid % HG_DIAG
    for s in range(HG_DIAG):
        dec = jnp.exp2(jnp.where(sub >= s, b2 - _tile_row(b2, HG_DIAG, s), NEG_INF))
        ps = q * _tile_row(kk, HG_DIAG, s) * dec
        o = o + _dot(ps.astype(BF16), ones) * _tile_row(vv, HG_DIAG, s)
    return o


def _hgrn_kernel(x_ref, s0_ref, g_ref, win_ref, wo_ref, gn_ref, lbp_ref,
                 o_ref, so_ref, st_ref, pj_ref, b2_ref, og_ref, *, chunk, nsub, layer):
    D = D_MODEL
    C = chunk
    t = pl.program_id(1)
    nt = pl.num_programs(1)

    @pl.when(t == 0)
    def _():
        for h in range(HG_HEADS):
            st_ref[h] = s0_ref[0, h].T

    x = x_ref[0]
    xg, rinv = _rms_split(x, g_ref[...])
    pj_ref[...] = _dot(xg, win_ref[...]) * rinv

    lbp = lbp_ref[...]
    e = jnp.exp(lbp - jnp.max(lbp, axis=0, keepdims=True))
    sm = e / jnp.sum(e, axis=0, keepdims=True)
    lb = jnp.zeros((1, D), F32)
    for i in range(1, layer + 1):
        lb = lb + sm[i:i + 1, :]
    log_lb = jnp.log(lb)
    log_1mlb = jnp.log1p(-lb)

    zf = pj_ref[:, D:2 * D]
    a = jnp.exp(-jnp.abs(zf))
    r = 1.0 / (1.0 + a)
    log_sig = jnp.minimum(zf, 0.0) - jnp.log(1.0 + a)
    bb = log_1mlb + log_sig
    logf = jnp.maximum(log_lb, bb) + jnp.log(1.0 + jnp.exp(-jnp.abs(log_lb - bb)))
    pj_ref[:, D:2 * D] = (1.0 - lb) * jnp.where(zf >= 0.0, a * r, r)
    zq = pj_ref[:, :D]
    pj_ref[:, :D] = zq * (1.0 / (1.0 + jnp.exp(-zq)))

    ri = lax.broadcasted_iota(jnp.int32, (C, C), 0)
    ci = lax.broadcasted_iota(jnp.int32, (C, C), 1)
    tril = ri >= ci
    tri = tril.astype(BF16)
    ends = []
    safe = None
    for c in range(nsub):
        hi, mid, lo = _split3(logf[c * C:(c + 1) * C, :])
        b2_c = (_dot(tri, hi) + _dot(tri, mid) + _dot(tri, lo)) * LOG2E
        b2_ref[c * C:(c + 1) * C, :] = b2_c
        b2_mid = b2_c[C // 2 - 1:C // 2, :]
        b2_end = b2_c[C - 1:C, :]
        ends.append(b2_end)
        ok = (jnp.min(b2_mid) >= -HG_SAFE_LOG2) & (jnp.min(b2_end - b2_mid) >= -HG_SAFE_LOG2)
        safe = ok if safe is None else safe & ok

    def finish(intra):
        for c in range(nsub):
            rows = slice(c * C, (c + 1) * C)
            for h in range(HG_HEADS):
                cols = slice(h * HG_DK, (h + 1) * HG_DK)
                q, kk, b2 = pj_ref[rows, cols], pj_ref[rows, D + h * HG_DK:D + (h + 1) * HG_DK], b2_ref[rows, cols]
                vv = pj_ref[rows, 2 * D + h * HG_DV:2 * D + (h + 1) * HG_DV]
                zg = pj_ref[rows, 3 * D + h * HG_DV:3 * D + (h + 1) * HG_DV]
                vb = vv.astype(BF16)
                st = st_ref[h]
                end = ends[c][:, cols]
                o = _dot((q * jnp.exp2(b2)).astype(BF16), st.T.astype(BF16))
                kdec = (kk * jnp.exp2(end - b2)).astype(BF16)
                st_ref[h] = st * jnp.exp2(end) + _dot_tn(vb, kdec)
                o = o + intra(q, kk, b2, vv, vb)
                on = o * lax.rsqrt(jnp.mean(o * o, axis=-1, keepdims=True) + EPS) * gn_ref[...]
                og_ref[rows, h * HG_DV:(h + 1) * HG_DV] = (on * (zg * (1.0 / (1.0 + jnp.exp(-zg))))).astype(BF16)
        o_ref[0] = x + _dot(og_ref[...], wo_ref[...])

    @pl.when(safe)
    def _():
        finish(lambda q, kk, b2, vv, vb: _hgrn_intra_fast(q, kk, b2, vb, tril))

    @pl.when(jnp.logical_not(safe))
    def _():
        rowid = lax.broadcasted_iota(jnp.int32, (C, 1), 0)
        ones = jnp.ones((HG_DK, HG_DV), BF16)
        levels = []
        hl = C // 2
        while hl >= HG_DIAG:
            levels.append((hl, (rowid % (2 * hl)) >= hl, (ri // (2 * hl)) == (ci // (2 * hl))))
            hl //= 2
        finish(lambda q, kk, b2, vv, vb: _hgrn_intra_safe(q, kk, b2, vv, vb, levels, rowid, ones))

    @pl.when(t == nt - 1)
    def _():
        for h in range(HG_HEADS):
            so_ref[0, h] = st_ref[h].T


def _hgrn(x, s0, g, w_in, w_o, gn, lbp, *, chunk, nsub, layer):
    S, L, D = x.shape
    tile = chunk * nsub
    assert L % tile == 0 and chunk % (2 * HG_DIAG) == 0
    tok = pl.BlockSpec((1, tile, D), lambda s, t: (s, t, 0))
    state = pl.BlockSpec((1, HG_HEADS, HG_DK, HG_DV), lambda s, t: (s, 0, 0, 0))
    return pl.pallas_call(
        functools.partial(_hgrn_kernel, chunk=chunk, nsub=nsub, layer=layer),
        grid=(S, L // tile),
        in_specs=[tok, state, _whole((1, D)), _whole((D, 4 * D)), _whole((D, D)), _whole((1, HG_DV)),
                  _whole((DEPTH, D))],
        out_specs=[tok, state],
        out_shape=[jax.ShapeDtypeStruct((S, L, D), F32),
                   jax.ShapeDtypeStruct((S, HG_HEADS, HG_DK, HG_DV), F32)],
        scratch_shapes=[pltpu.VMEM((HG_HEADS, HG_DV, HG_DK), F32),
                        pltpu.VMEM((tile, 4 * D), F32),
                        pltpu.VMEM((tile, D), F32),
                        pltpu.VMEM((tile, D), BF16)],
        compiler_params=_params(),
        name="hgrn",
    )(x, s0, g, w_in, w_o, gn, lbp)


MLP_TILE = 512
POOL_TILE = (512, 64)
ATTN_TILE = (128, 64)
ATTN_NSUB = (2, 1)
HGRN_CHUNK = (128, 64)
HGRN_NSUB = (4, 1)


def kernel(x_prompt, x_sample, cache_pool, cache_attn_k, cache_attn_v, state_hgrn, norm_mix, norm_ffn,
           norm_final, pool_w, pool_scale, attn_wqkv, attn_wo, attn_rel_bias, hgrn_w_in, hgrn_w_o,
           hgrn_gnorm, hgrn_lower_bounds, mlp_w1, mlp_w2):
    B, L, D = x_prompt.shape
    BS, LS, _ = x_sample.shape
    xp = x_prompt
    xs = x_sample
    gf = norm_final.reshape(1, D)
    def next_casts(i):
        if i + 1 >= DEPTH:
            return []
        kind, j = (i + 1) % N_MIXERS, (i + 1) // N_MIXERS
        casts = [(mlp_w1, i + 1), (mlp_w2, i + 1)]
        if kind == 1:
            casts += [(attn_wqkv, j), (attn_wo, j)]
        elif kind == 2:
            casts += [(hgrn_w_in, j), (hgrn_w_o, j)]
        return casts

    w1 = w2 = None
    mix_w = []
    pw_all = pool_w.astype(BF16)

    pool_p, pool_s, k_p, k_s, v_p, v_s, h_p, h_s = [], [], [], [], [], [], [], []
    for i in range(DEPTH):
        kind, j = i % N_MIXERS, i // N_MIXERS
        g = norm_mix[i].reshape(1, D)
        gm = norm_ffn[i].reshape(1, D)
        last = i == DEPTH - 1
        if kind == 0:
            sc = pool_scale[j].reshape(1, D)
            hist_p = jnp.zeros((B, POOL_HALO, D), F32)
            hist_s = jnp.pad(cache_pool[j], ((0, 0), (1, 0), (0, 0)))
            first = [(mlp_w1, 0), (mlp_w2, 0)] if i == 0 else []
            xs, cs, cast = _pool(xs, hist_s, g, pw_all, sc, first, pool_layer=j, tile=POOL_TILE[1], p0=PAST_LEN)
            if cast:
                w1, w2 = cast
            xp, cp, xs, cast = _pool_mlp(xp, xs.reshape(1, BS * LS, D), hist_p, g, pw_all, sc, gm, w1, w2, gf,
                                         next_casts(i), pool_layer=j, tile=MLP_TILE, p0=0, final_norm=last)
            pool_p.append(cp)
            pool_s.append(cs)
        else:
            if kind == 1:
                wqkv, wo = mix_w
                keep_p = min(ATTN_WINDOW, L)
                zeros = jnp.zeros((B, ATTN_WINDOW, ATTN_HEADS, ATTN_HD), F32)
                xp, kp, vp = _attn(xp, zeros, zeros, g, wqkv, wo, attn_rel_bias[j],
                                   tile=ATTN_TILE[0], nsub=ATTN_NSUB[0], keep=keep_p, mask_start=True)
                ck = cache_attn_k[j]
                cv = cache_attn_v[j]
                xs, ks, vs = _attn(xs, ck, cv, g, wqkv, wo, attn_rel_bias[j],
                                   tile=ATTN_TILE[1], nsub=ATTN_NSUB[1], keep=LS, mask_start=False)
                k_p.append(kp.reshape(B, keep_p, ATTN_HEADS, ATTN_HD))
                v_p.append(vp.reshape(B, keep_p, ATTN_HEADS, ATTN_HD))
                k_s.append(ks.reshape(BS, LS, ATTN_HEADS, ATTN_HD))
                v_s.append(vs.reshape(BS, LS, ATTN_HEADS, ATTN_HD))
            else:
                w_in, w_o = mix_w
                gn = hgrn_gnorm[j].reshape(1, HG_DV)
                s0 = jnp.zeros((B, HG_HEADS, HG_DK, HG_DV), F32)
                xp, sp = _hgrn(xp, s0, g, w_in, w_o, gn, hgrn_lower_bounds, chunk=HGRN_CHUNK[0],
                               nsub=HGRN_NSUB[0], layer=i)
                xs, ss = _hgrn(xs, state_hgrn[j], g, w_in, w_o, gn, hgrn_lower_bounds, chunk=HGRN_CHUNK[1],
                               nsub=HGRN_NSUB[1], layer=i)
                h_p.append(sp)
                h_s.append(ss)
            xp, xs, cast = _mlp(xp.reshape(1, B * L, D), xs.reshape(1, BS * LS, D), gm, w1, w2, gf,
                                next_casts(i), tile=MLP_TILE, final_norm=last)
            xp = xp.reshape(B, L, D)
        xs = xs.reshape(BS, LS, D)
        if cast:
            w1, w2, mix_w = cast[0], cast[1], cast[2:]

    return (xp, xs, jnp.stack(pool_p), jnp.stack(pool_s), jnp.stack(k_p), jnp.stack(k_s),
            jnp.stack(v_p), jnp.stack(v_s), jnp.stack(h_p), jnp.stack(h_s))
```

```python
import functools
import math

import jax
import jax.numpy as jnp
from jax import lax
from jax.experimental import pallas as pl
from jax.experimental.pallas import tpu as pltpu

F32 = jnp.float32
BF16 = jnp.bfloat16

D_MODEL = 1024
DEPTH = 4
CHUNK = 64
N_MIXERS = 3
PAST_LEN = 2048
EPS = 1e-6
POOL_WINDOWS = (2, 4, 8, 16)
POOL_GROUPS = 4
POOL_GC = D_MODEL // POOL_GROUPS
POOL_HIST = max(POOL_WINDOWS) - 1
POOL_HALO = POOL_HIST + 1
ATTN_HEADS = 16
ATTN_HD = D_MODEL // ATTN_HEADS
N_PAST_CHUNKS = 8
ATTN_WINDOW = N_PAST_CHUNKS * CHUNK
REL_MIN = -(CHUNK - 1)
REL_MAX = 256
REL_SIZE = REL_MAX - REL_MIN + 1
HEAD_PAIRS = ATTN_HEADS // 2
PAIR_W = 2 * ATTN_HD
HG_HEADS = 8
HG_DK = D_MODEL // HG_HEADS
HG_DV = D_MODEL // HG_HEADS
HG_DIAG = 8
HG_SAFE_LOG2 = 100.0
LOG2E = math.log2(math.e)
D_FF = 4 * D_MODEL

V7X_VMEM_BYTES = 64 * 1024 * 1024
VMEM_LIMIT_BYTES = V7X_VMEM_BYTES - 8 * 1024 * 1024

NEG_INF = float("-inf")


def _params():
    return pltpu.CompilerParams(dimension_semantics=("arbitrary", "arbitrary"),
                                vmem_limit_bytes=VMEM_LIMIT_BYTES)


def _whole(shape, single_buffer=False):
    zeros = (0,) * len(shape)
    kw = dict(pipeline_mode=pl.Buffered(1)) if single_buffer else {}
    return pl.BlockSpec(shape, lambda s, t: zeros, **kw)


def _layer(shape, i):
    zeros = (0,) * len(shape)
    return pl.BlockSpec((None,) + tuple(shape), lambda s, t: (i,) + zeros)


def _cast_specs(w, layer, nchunk):
    _, rows, cols = w.shape
    assert rows % nchunk == 0 and (rows // nchunk) % 16 == 0
    cr = rows // nchunk
    src = pl.BlockSpec((None, cr, cols), lambda s, t: (layer, jnp.minimum(t, nchunk - 1), 0))
    dst = pl.BlockSpec((cr, cols), lambda s, t: (jnp.minimum(t, nchunk - 1), 0))
    return src, dst, jax.ShapeDtypeStruct((rows, cols), BF16)


def _cast_chunks(src_refs, dst_refs):
    for s_ref, d_ref in zip(src_refs, dst_refs):
        d_ref[...] = s_ref[...].astype(BF16)


def _rms(x, g):
    ms = jnp.mean(x * x, axis=-1, keepdims=True)
    return x * lax.rsqrt(ms + EPS) * g


def _rms_split(x, g):
    r = lax.rsqrt(jnp.mean(x * x, axis=-1, keepdims=True) + EPS)
    return (x * g).astype(BF16), r


def _dot(a, b):
    return jnp.dot(a, b, preferred_element_type=F32)


def _dot_tn(a, b):
    return lax.dot_general(a, b, (((0,), (0,)), ((), ())), preferred_element_type=F32)


def _split3(x):
    hi = x.astype(BF16)
    r1 = x - hi.astype(F32)
    mid = r1.astype(BF16)
    lo = (r1 - mid.astype(F32)).astype(BF16)
    return hi, mid, lo


def _mlp_apply(x, g, w1_ref, w2_ref, gf, final_norm):
    xg, r = _rms_split(x, g)
    h = _dot(xg, w1_ref[...])
    h = jnp.square(jnp.maximum(h, 0.0)).astype(BF16)
    y = x + (r * r) * _dot(h, w2_ref[...])
    if final_norm:
        y = _rms(y, gf)
    return y


def _mlp_kernel(xp_ref, xs_ref, g_ref, w1_ref, w2_ref, gf_ref, *rest, final_norm, n_cast):
    op_ref, os_ref = rest[n_cast:n_cast + 2]
    _cast_chunks(rest[:n_cast], rest[n_cast + 2:])
    t = pl.program_id(1)
    last = pl.num_programs(1) - 1

    @pl.when(t < last)
    def _():
        op_ref[0] = _mlp_apply(xp_ref[0], g_ref[...], w1_ref, w2_ref, gf_ref[...], final_norm)

    @pl.when(t == last)
    def _():
        os_ref[0] = _mlp_apply(xs_ref[0], g_ref[...], w1_ref, w2_ref, gf_ref[...], final_norm)


def _mlp(xp, xs, g, w1, w2, gf, casts, *, tile, final_norm):
    _, Lp, D = xp.shape
    _, Ls, _ = xs.shape
    ntp = Lp // tile
    assert Lp % tile == 0
    tok = pl.BlockSpec((1, tile, D), lambda s, t: (0, jnp.minimum(t, ntp - 1), 0))
    smp = pl.BlockSpec((1, Ls, D), lambda s, t: (0, 0, 0))
    cs = [_cast_specs(w, layer, ntp) for w, layer in casts]
    outs = pl.pallas_call(
        functools.partial(_mlp_kernel, final_norm=final_norm, n_cast=len(casts)),
        grid=(1, ntp + 1),
        in_specs=[tok, smp, _whole((1, D)), _whole((D, D_FF)), _whole((D_FF, D)), _whole((1, D))]
        + [c[0] for c in cs],
        out_specs=[tok, smp] + [c[1] for c in cs],
        out_shape=[jax.ShapeDtypeStruct((1, Lp, D), F32), jax.ShapeDtypeStruct((1, Ls, D), F32)]
        + [c[2] for c in cs],
        compiler_params=_params(),
        name="mlp",
    )(xp, xs, g, w1, w2, gf, *[w for w, _ in casts])
    return outs[0], outs[1], list(outs[2:])


def _pool_group(u, pos, ext_ref, w_ref, gi, tile):
    w = POOL_WINDOWS[gi]
    c0, c1 = gi * POOL_GC, (gi + 1) * POOL_GC
    win = u[:, c0:c1]
    for j in range(1, w):
        win = win + ext_ref[POOL_HALO - j:POOL_HALO - j + tile, c0:c1]
    cnt = jnp.minimum(pos + 1, w).astype(F32)
    diff = win / cnt - u[:, c0:c1]
    return _dot(diff.astype(BF16), w_ref[gi])


def _pool_apply(x, pos0, ext_ref, g, w_ref, sc, tile):
    u = _rms(x, g)
    ext_ref[POOL_HALO:POOL_HALO + tile, :] = u
    pos = pos0 + lax.broadcasted_iota(jnp.int32, (tile, 1), 0)
    ys = [_pool_group(u, pos, ext_ref, w_ref, gi, tile) for gi in range(POOL_GROUPS)]
    y = jnp.concatenate(ys, axis=1) * sc
    return x + y, ext_ref[tile:tile + POOL_HALO, :]


def _pool_kernel(x_ref, hist_ref, g_ref, w_ref, sc_ref, o_ref, cache_ref, ext_ref, *, tile, p0):
    t = pl.program_id(1)

    @pl.when(t == 0)
    def _():
        ext_ref[0:POOL_HALO, :] = hist_ref[0]

    out, tail = _pool_apply(x_ref[0], p0 + t * tile, ext_ref, g_ref[...], w_ref, sc_ref[...], tile)
    o_ref[0] = out
    cache_ref[0] = tail
    ext_ref[0:POOL_HALO, :] = tail


def _pool_mlp_kernel(x0_ref, xn_ref, xs_ref, hist_ref, gm_ref, pw_ref, sc_ref, gffn_ref, w1_ref, w2_ref, gf_ref,
                     *rest, tile, p0, final_norm, n_cast):
    o_ref, cache_ref, os_ref = rest[n_cast:n_cast + 3]
    ext_ref, x1_ref = rest[2 * n_cast + 3:]
    _cast_chunks(rest[:n_cast], rest[n_cast + 3:2 * n_cast + 3])
    t = pl.program_id(1)
    nt = pl.num_programs(1) - 1

    @pl.when(t == nt)
    def _():
        os_ref[0] = _mlp_apply(xs_ref[0], gffn_ref[...], w1_ref, w2_ref, gf_ref[...], final_norm)

    @pl.when(t == 0)
    def _():
        ext_ref[0:POOL_HALO, :] = hist_ref[0]
        out, tail = _pool_apply(x0_ref[0], p0, ext_ref, gm_ref[...], pw_ref, sc_ref[...], tile)
        x1_ref[0] = out
        ext_ref[0:POOL_HALO, :] = tail

    @pl.when(t < nt)
    def _():
        x1 = x1_ref[lax.rem(t, 2)]
        v, r = _rms_split(x1, gffn_ref[...])
        halo = ext_ref[0:POOL_HALO, :]
        xn = xn_ref[0]
        u = _rms(xn, gm_ref[...])
        ext_ref[POOL_HALO:POOL_HALO + tile, :] = u
        pos = p0 + (t + 1) * tile + lax.broadcasted_iota(jnp.int32, (tile, 1), 0)
        fc = D_FF // POOL_GROUPS
        acc = None
        ys = []
        for c in range(POOL_GROUPS):
            h = _dot(v, w1_ref[:, c * fc:(c + 1) * fc])
            h = jnp.square(jnp.maximum(h, 0.0)).astype(BF16)
            part = _dot(h, w2_ref[c * fc:(c + 1) * fc, :])
            acc = part if acc is None else acc + part
            ys.append(_pool_group(u, pos, ext_ref, pw_ref, c, tile))
        acc = x1 + (r * r) * acc
        if final_norm:
            acc = _rms(acc, gf_ref[...])
        o_ref[0] = acc

        x1_ref[lax.rem(t + 1, 2)] = xn + jnp.concatenate(ys, axis=1) * sc_ref[...]
        halo = jnp.where(t + 1 < nt, ext_ref[tile:tile + POOL_HALO, :], halo)
        ext_ref[0:POOL_HALO, :] = halo
        cache_ref[0] = halo


def _pool_mlp(x, xs, hist, gm, pw, sc, gffn, w1, w2, gf, casts, *, pool_layer, tile, p0, final_norm):
    S, L, D = x.shape
    _, Ls, _ = xs.shape
    nt = L // tile
    assert S == 1 and L % tile == 0 and tile >= POOL_HALO
    first = pl.BlockSpec((1, tile, D), lambda s, t: (s, 0, 0))
    nxt = pl.BlockSpec((1, tile, D), lambda s, t: (s, jnp.minimum(t + 1, nt - 1), 0))
    tok = pl.BlockSpec((1, tile, D), lambda s, t: (s, jnp.minimum(t, nt - 1), 0))
    smp = pl.BlockSpec((1, Ls, D), lambda s, t: (0, 0, 0))
    seq = pl.BlockSpec((1, POOL_HALO, D), lambda s, t: (s, 0, 0))
    cs = [_cast_specs(w, layer, nt) for w, layer in casts]
    outs = pl.pallas_call(
        functools.partial(_pool_mlp_kernel, tile=tile, p0=p0, final_norm=final_norm, n_cast=len(casts)),
        grid=(S, nt + 1),
        in_specs=[first, nxt, smp, seq, _whole((1, D)), _layer((POOL_GROUPS, POOL_GC, POOL_GC), pool_layer),
                  _whole((1, D)), _whole((1, D)),
                  _whole((D, D_FF), single_buffer=True), _whole((D_FF, D), single_buffer=True), _whole((1, D))]
        + [c[0] for c in cs],
        out_specs=[tok, seq, smp] + [c[1] for c in cs],
        out_shape=[jax.ShapeDtypeStruct((S, L, D), F32), jax.ShapeDtypeStruct((S, POOL_HALO, D), F32),
                   jax.ShapeDtypeStruct((1, Ls, D), F32)] + [c[2] for c in cs],
        scratch_shapes=[pltpu.VMEM((POOL_HALO + tile, D), F32), pltpu.VMEM((2, tile, D), F32)],
        compiler_params=_params(),
        name="pool_mlp",
    )(x, x, xs, hist, gm, pw, sc, gffn, w1, w2, gf, *[w for w, _ in casts])
    return outs[0], outs[1][:, 1:, :], outs[2], list(outs[3:])


def _pool(x, hist, g, w, sc, *, pool_layer, tile, p0):
    S, L, D = x.shape
    assert L % tile == 0 and tile >= POOL_HALO
    tok = pl.BlockSpec((1, tile, D), lambda s, t: (s, t, 0))
    seq = pl.BlockSpec((1, POOL_HALO, D), lambda s, t: (s, 0, 0))
    out, cache = pl.pallas_call(
        functools.partial(_pool_kernel, tile=tile, p0=p0),
        grid=(S, L // tile),
        in_specs=[tok, seq, _whole((1, D)), _layer((POOL_GROUPS, POOL_GC, POOL_GC), pool_layer), _whole((1, D))],
        out_specs=[tok, seq],
        out_shape=[jax.ShapeDtypeStruct((S, L, D), F32), jax.ShapeDtypeStruct((S, POOL_HALO, D), F32)],
        scratch_shapes=[pltpu.VMEM((POOL_HALO + tile, D), F32)],
        compiler_params=_params(),
        name="pool",
    )(x, hist, g, w, sc)
    return out, cache[:, 1:, :]


def _build_attn_bias(rb_ref, bias_ref, *, tile):
    nb = ATTN_WINDOW // tile + 1
    w2 = 2 * tile
    col = lax.broadcasted_iota(jnp.int32, (REL_SIZE, nb * w2), 1)
    row = lax.broadcasted_iota(jnp.int32, (REL_SIZE, nb * w2), 0)
    ip = col % w2
    i = jnp.where(ip < tile, ip, ip - w2)
    idx = jnp.clip((col // w2) * tile - i, REL_MIN, REL_MAX) - REL_MIN
    sel = jnp.where(row == idx, 1.0, 0.0).astype(BF16)
    hi, mid, lo = _split3(rb_ref[...])
    gen = _dot(hi, sel) + _dot(mid, sel) + _dot(lo, sel)

    r = lax.broadcasted_iota(jnp.int32, (tile, w2), 0)
    lane = lax.broadcasted_iota(jnp.int32, (tile, w2), 1)
    first = lane < tile
    for d in range(nb):
        dchunk = (d * tile) // CHUNK + r // CHUNK - (lane % tile) // CHUNK
        ok = (dchunk >= 0) & (dchunk <= N_PAST_CHUNKS)
        for p in range(HEAD_PAIRS):
            ga = jnp.broadcast_to(gen[2 * p:2 * p + 1, d * w2:(d + 1) * w2], (tile, w2))
            gb = jnp.broadcast_to(gen[2 * p + 1:2 * p + 2, d * w2:(d + 1) * w2], (tile, w2))
            ta = pltpu.roll(ga, 0, 1, stride=1, stride_axis=0)
            tb = pltpu.roll(gb, tile, 1, stride=1, stride_axis=0)
            bias_ref[p, d] = jnp.where(ok, jnp.where(first, ta, tb) * LOG2E, NEG_INF)


def _attn_kernel(x_ref, hk_ref, hv_ref, g_ref, wqkv_ref, wo_ref, rb_ref,
                 o_ref, ko_ref, vo_ref, kt_ring, v_ring, bias_ref, *, tile, nsub, mask_start):
    D = D_MODEL
    nb = ATTN_WINDOW // tile + 1
    ring = ATTN_WINDOW // tile + nsub
    t = pl.program_id(1)

    row_lo = lax.broadcasted_iota(jnp.int32, (PAIR_W, tile), 0) < ATTN_HD
    lane_lo = lax.broadcasted_iota(jnp.int32, (tile, PAIR_W), 1) < ATTN_HD
    first = lax.broadcasted_iota(jnp.int32, (tile, 2 * tile), 1) < tile

    @pl.when((pl.program_id(0) == 0) & (t == 0))
    def _():
        _build_attn_bias(rb_ref, bias_ref, tile=tile)

    rr = lax.broadcasted_iota(jnp.int32, (2 * tile, PAIR_W), 0)
    cc = lax.broadcasted_iota(jnp.int32, (2 * tile, PAIR_W), 1)
    sum_cols = jnp.where(((cc == 0) & (rr < tile)) | ((cc == 1) & (rr >= tile)), 1.0, 0.0).astype(BF16)

    def store_block(slot, k, v):
        kt = k.T.astype(BF16)
        vb = v.astype(BF16)
        zk = jnp.zeros((PAIR_W, tile), BF16)
        zv = jnp.zeros((tile, PAIR_W), BF16)
        for p in range(HEAD_PAIRS):
            ktp = kt[p * PAIR_W:(p + 1) * PAIR_W, :]
            kt_ring[slot, p] = jnp.concatenate(
                [jnp.where(row_lo, ktp, zk), jnp.where(row_lo, zk, ktp)], axis=1)
            vp = vb[:, p * PAIR_W:(p + 1) * PAIR_W]
            v_ring[slot, p, :, 0:PAIR_W] = jnp.concatenate(
                [jnp.where(lane_lo, vp, zv), jnp.where(lane_lo, zv, vp)], axis=0)
            v_ring[slot, p, :, PAIR_W:2 * PAIR_W] = sum_cols

    @pl.when(t == 0)
    def _():
        for i in range(nb - 1):
            store_block(i + nsub, hk_ref[0, i * tile:(i + 1) * tile, :], hv_ref[0, i * tile:(i + 1) * tile, :])

    x = x_ref[0]
    xg, r = _rms_split(x, g_ref[...])
    qkv = _dot(xg, wqkv_ref[...])
    q = (qkv[:, :D] * (r * (LOG2E * ATTN_HD ** -0.5))).astype(BF16)
    k = qkv[:, D:2 * D] * r
    v = qkv[:, 2 * D:] * r
    ko_ref[0] = k
    vo_ref[0] = v
    for j in range(nsub):
        store_block(lax.rem(t * nsub + j, ring), k[j * tile:(j + 1) * tile, :], v[j * tile:(j + 1) * tile, :])

    def slot_of(j, d):
        return lax.rem(t * nsub + j + ring - d, ring)

    def score_block(j, p, d):
        s = _dot(q[j * tile:(j + 1) * tile, p * PAIR_W:(p + 1) * PAIR_W], kt_ring[slot_of(j, d), p]) + bias_ref[p, d]
        if mask_start and d > j:
            s = s + jnp.where(t * nsub + j >= d, 0.0, NEG_INF)
        return s

    def row_max(scores):
        mx = functools.reduce(jnp.maximum, scores)
        m_a = jnp.max(mx[:, :tile], axis=-1, keepdims=True)
        m_b = jnp.max(mx[:, tile:], axis=-1, keepdims=True)
        return jnp.where(first, m_a, m_b)

    units = [(j, p) for j in range(nsub) for p in range(HEAD_PAIRS)]
    outs = [[] for _ in range(nsub)]
    nxt = [score_block(0, 0, d) for d in range(nb)]
    for n, (j, p) in enumerate(units):
        scores, m = nxt, row_max(nxt)
        if n + 1 < len(units):
            nxt = [score_block(units[n + 1][0], units[n + 1][1], d) for d in range(nb)]
        acc = jnp.zeros((tile, 2 * PAIR_W), F32)
        for d in range(nb):
            pr = jnp.exp2(scores[d] - m)
            acc = acc + _dot(pr.astype(BF16), v_ring[slot_of(j, d), p])
        l_a = acc[:, PAIR_W:PAIR_W + 1]
        l_b = acc[:, PAIR_W + 1:PAIR_W + 2]
        inv = jnp.where(lane_lo, 1.0 / l_a, 1.0 / l_b)
        outs[j].append((acc[:, :PAIR_W] * inv).astype(BF16))
    o = jnp.concatenate([jnp.concatenate(row, axis=1) for row in outs], axis=0)
    o_ref[0] = x + _dot(o, wo_ref[...])


def _attn(x, hist_k, hist_v, g, wqkv, wo, rel_bias, *, tile, nsub, keep, mask_start):
    S, L, D = x.shape
    step = tile * nsub
    nb = ATTN_WINDOW // tile + 1
    nt = L // step
    nkeep = keep // step
    assert L % step == 0 and tile % CHUNK == 0 and ATTN_WINDOW % tile == 0 and keep % step == 0
    tok = pl.BlockSpec((1, step, D), lambda s, t: (s, t, 0))
    hist = pl.BlockSpec((1, ATTN_WINDOW, D), lambda s, t: (s, 0, 0))
    kv_out = pl.BlockSpec((1, step, D), lambda s, t: (s, jnp.maximum(t - (nt - nkeep), 0), 0))
    return pl.pallas_call(
        functools.partial(_attn_kernel, tile=tile, nsub=nsub, mask_start=mask_start),
        grid=(S, nt),
        in_specs=[tok, hist, hist, _whole((1, D)), _whole((D, 3 * D)), _whole((D, D)),
                  _whole((ATTN_HEADS, REL_SIZE))],
        out_specs=[tok, kv_out, kv_out],
        out_shape=[jax.ShapeDtypeStruct((S, L, D), F32),
                   jax.ShapeDtypeStruct((S, keep, D), F32),
                   jax.ShapeDtypeStruct((S, keep, D), F32)],
        scratch_shapes=[pltpu.VMEM((nb - 1 + nsub, HEAD_PAIRS, PAIR_W, 2 * tile), BF16),
                        pltpu.VMEM((nb - 1 + nsub, HEAD_PAIRS, 2 * tile, 2 * PAIR_W), BF16),
                        pltpu.VMEM((HEAD_PAIRS, nb, tile, 2 * tile), F32)],
        compiler_params=_params(),
        name="attn",
    )(x, hist_k, hist_v, g, wqkv, wo, rel_bias)


def _tile_row(x, n, s):
    C, W = x.shape
    xr = x.reshape(C // n, n, W)
    return jnp.broadcast_to(xr[:, s:s + 1, :], (C // n, n, W)).reshape(C, W)


def _hgrn_intra_fast(q, kk, b2, vb, tril):
    C = q.shape[0]
    bmid = b2[C // 2 - 1:C // 2, :]
    qt = (q * jnp.exp2(b2 - bmid)).astype(BF16)
    kt = (kk * jnp.exp2(bmid - b2)).T.astype(BF16)
    amat = jnp.where(tril, _dot(qt, kt), 0.0)
    return _dot(amat.astype(BF16), vb)


def _hgrn_intra_safe(q, kk, b2, vv, vb, levels, rowid, ones):
    C = q.shape[0]
    amat = jnp.zeros((C, C), F32)
    for hl, upper, same in levels:
        bref = _tile_row(b2, 2 * hl, hl - 1)
        qt = q * jnp.exp2(jnp.where(upper, b2 - bref, NEG_INF))
        kt = kk * jnp.exp2(jnp.where(upper, NEG_INF, bref - b2))
        amat = amat + jnp.where(same, _dot(qt.astype(BF16), kt.T.astype(BF16)), 0.0)
    o = _dot(amat.astype(BF16), vb)
    sub = rowid % HG_DIAG
    for s in range(HG_DIAG):
        dec = jnp.exp2(jnp.where(sub >= s, b2 - _tile_row(b2, HG_DIAG, s), NEG_INF))
        ps = q * _tile_row(kk, HG_DIAG, s) * dec
        o = o + _dot(ps.astype(BF16), ones) * _tile_row(vv, HG_DIAG, s)
    return o


def _hgrn_kernel(x_ref, s0_ref, g_ref, win_ref, wo_ref, gn_ref, lbp_ref,
                 o_ref, so_ref, st_ref, pj_ref, b2_ref, og_ref, *, chunk, nsub, layer, independent):
    D = D_MODEL
    C = chunk
    t = pl.program_id(1)
    nt = pl.num_programs(1)
    nstate = nsub if independent else 1

    @pl.when(t == 0)
    def _():
        for c in range(nstate):
            for h in range(HG_HEADS):
                st_ref[c * HG_HEADS + h] = s0_ref[c, h].T

    x = x_ref[...].reshape(nsub * C, D)
    xg, rinv = _rms_split(x, g_ref[...])
    pj_ref[...] = _dot(xg, win_ref[...]) * rinv

    lbp = lbp_ref[...]
    e = jnp.exp(lbp - jnp.max(lbp, axis=0, keepdims=True))
    sm = e / jnp.sum(e, axis=0, keepdims=True)
    lb = jnp.zeros((1, D), F32)
    for i in range(1, layer + 1):
        lb = lb + sm[i:i + 1, :]
    log_lb = jnp.log(lb)
    log_1mlb = jnp.log1p(-lb)

    zf = pj_ref[:, D:2 * D]
    a = jnp.exp(-jnp.abs(zf))
    r = 1.0 / (1.0 + a)
    log_sig = jnp.minimum(zf, 0.0) - jnp.log(1.0 + a)
    bb = log_1mlb + log_sig
    logf = jnp.maximum(log_lb, bb) + jnp.log(1.0 + jnp.exp(-jnp.abs(log_lb - bb)))
    pj_ref[:, D:2 * D] = (1.0 - lb) * jnp.where(zf >= 0.0, a * r, r)
    zq = pj_ref[:, :D]
    pj_ref[:, :D] = zq * (1.0 / (1.0 + jnp.exp(-zq)))

    ri = lax.broadcasted_iota(jnp.int32, (C, C), 0)
    ci = lax.broadcasted_iota(jnp.int32, (C, C), 1)
    tril = ri >= ci
    tri = tril.astype(BF16)
    ends = []
    safe = None
    for c in range(nsub):
        hi, mid, lo = _split3(logf[c * C:(c + 1) * C, :])
        b2_c = (_dot(tri, hi) + _dot(tri, mid) + _dot(tri, lo)) * LOG2E
        b2_ref[c * C:(c + 1) * C, :] = b2_c
        b2_mid = b2_c[C // 2 - 1:C // 2, :]
        b2_end = b2_c[C - 1:C, :]
        ends.append(b2_end)
        ok = (jnp.min(b2_mid) >= -HG_SAFE_LOG2) & (jnp.min(b2_end - b2_mid) >= -HG_SAFE_LOG2)
        safe = ok if safe is None else safe & ok

    def finish(intra):
        for c in range(nsub):
            rows = slice(c * C, (c + 1) * C)
            for h in range(HG_HEADS):
                cols = slice(h * HG_DK, (h + 1) * HG_DK)
                q, kk, b2 = pj_ref[rows, cols], pj_ref[rows, D + h * HG_DK:D + (h + 1) * HG_DK], b2_ref[rows, cols]
                vv = pj_ref[rows, 2 * D + h * HG_DV:2 * D + (h + 1) * HG_DV]
                zg = pj_ref[rows, 3 * D + h * HG_DV:3 * D + (h + 1) * HG_DV]
                vb = vv.astype(BF16)
                sidx = c * HG_HEADS + h if independent else h
                st = st_ref[sidx]
                end = ends[c][:, cols]
                o = _dot((q * jnp.exp2(b2)).astype(BF16), st.T.astype(BF16))
                kdec = (kk * jnp.exp2(end - b2)).astype(BF16)
                st_ref[sidx] = st * jnp.exp2(end) + _dot_tn(vb, kdec)
                o = o + intra(q, kk, b2, vv, vb)
                on = o * lax.rsqrt(jnp.mean(o * o, axis=-1, keepdims=True) + EPS) * gn_ref[...]
                og_ref[rows, h * HG_DV:(h + 1) * HG_DV] = (on * (zg * (1.0 / (1.0 + jnp.exp(-zg))))).astype(BF16)
        o_ref[...] = (x + _dot(og_ref[...], wo_ref[...])).reshape(o_ref.shape)

    @pl.when(safe)
    def _():
        finish(lambda q, kk, b2, vv, vb: _hgrn_intra_fast(q, kk, b2, vb, tril))

    @pl.when(jnp.logical_not(safe))
    def _():
        rowid = lax.broadcasted_iota(jnp.int32, (C, 1), 0)
        ones = jnp.ones((HG_DK, HG_DV), BF16)
        levels = []
        hl = C // 2
        while hl >= HG_DIAG:
            levels.append((hl, (rowid % (2 * hl)) >= hl, (ri // (2 * hl)) == (ci // (2 * hl))))
            hl //= 2
        finish(lambda q, kk, b2, vv, vb: _hgrn_intra_safe(q, kk, b2, vv, vb, levels, rowid, ones))

    @pl.when(t == nt - 1)
    def _():
        for c in range(nstate):
            for h in range(HG_HEADS):
                so_ref[c, h] = st_ref[c * HG_HEADS + h].T


def _hgrn(x, s0, g, w_in, w_o, gn, lbp, *, chunk, nsub, layer):
    S, L, D = x.shape
    independent = L == chunk
    tile = chunk * nsub
    assert chunk % (2 * HG_DIAG) == 0 and (S % nsub == 0 if independent else L % tile == 0)
    if independent:
        grid = (S // nsub, 1)
        tok = pl.BlockSpec((nsub, chunk, D), lambda s, t: (s, 0, 0))
        state = pl.BlockSpec((nsub, HG_HEADS, HG_DK, HG_DV), lambda s, t: (s, 0, 0, 0))
    else:
        grid = (S, L // tile)
        tok = pl.BlockSpec((1, tile, D), lambda s, t: (s, t, 0))
        state = pl.BlockSpec((1, HG_HEADS, HG_DK, HG_DV), lambda s, t: (s, 0, 0, 0))
    nstate = nsub if independent else 1
    return pl.pallas_call(
        functools.partial(_hgrn_kernel, chunk=chunk, nsub=nsub, layer=layer, independent=independent),
        grid=grid,
        in_specs=[tok, state, _whole((1, D)), _whole((D, 4 * D)), _whole((D, D)), _whole((1, HG_DV)),
                  _whole((DEPTH, D))],
        out_specs=[tok, state],
        out_shape=[jax.ShapeDtypeStruct((S, L, D), F32),
                   jax.ShapeDtypeStruct((S, HG_HEADS, HG_DK, HG_DV), F32)],
        scratch_shapes=[pltpu.VMEM((nstate * HG_HEADS, HG_DV, HG_DK), F32),
                        pltpu.VMEM((tile, 4 * D), F32),
                        pltpu.VMEM((tile, D), F32),
                        pltpu.VMEM((tile, D), BF16)],
        compiler_params=_params(),
        name="hgrn",
    )(x, s0, g, w_in, w_o, gn, lbp)


MLP_TILE = 512
POOL_TILE = (512, 64)
ATTN_TILE = (128, 64)
ATTN_NSUB = (2, 1)
HGRN_CHUNK = (128, 64)
HGRN_NSUB = (4, 4)


def kernel(x_prompt, x_sample, cache_pool, cache_attn_k, cache_attn_v, state_hgrn, norm_mix, norm_ffn,
           norm_final, pool_w, pool_scale, attn_wqkv, attn_wo, attn_rel_bias, hgrn_w_in, hgrn_w_o,
           hgrn_gnorm, hgrn_lower_bounds, mlp_w1, mlp_w2):
    B, L, D = x_prompt.shape
    BS, LS, _ = x_sample.shape
    xp = x_prompt
    xs = x_sample
    gf = norm_final.reshape(1, D)
    def next_casts(i):
        if i + 1 >= DEPTH:
            return []
        kind, j = (i + 1) % N_MIXERS, (i + 1) // N_MIXERS
        casts = [(mlp_w1, i + 1), (mlp_w2, i + 1)]
        if kind == 1:
            casts += [(attn_wqkv, j), (attn_wo, j)]
        elif kind == 2:
            casts += [(hgrn_w_in, j), (hgrn_w_o, j)]
        return casts

    w1 = mlp_w1[0].astype(BF16)
    w2 = mlp_w2[0].astype(BF16)
    mix_w = []
    pw_all = pool_w.astype(BF16)

    pool_p, pool_s, k_p, k_s, v_p, v_s, h_p, h_s = [], [], [], [], [], [], [], []
    for i in range(DEPTH):
        kind, j = i % N_MIXERS, i // N_MIXERS
        g = norm_mix[i].reshape(1, D)
        gm = norm_ffn[i].reshape(1, D)
        last = i == DEPTH - 1
        if kind == 0:
            sc = pool_scale[j].reshape(1, D)
            hist_p = jnp.zeros((B, POOL_HALO, D), F32)
            hist_s = jnp.pad(cache_pool[j], ((0, 0), (1, 0), (0, 0)))
            xs, cs = _pool(xs, hist_s, g, pw_all, sc, pool_layer=j, tile=POOL_TILE[1], p0=PAST_LEN)
            xp, cp, xs, cast = _pool_mlp(xp, xs.reshape(1, BS * LS, D), hist_p, g, pw_all, sc, gm, w1, w2, gf,
                                         next_casts(i), pool_layer=j, tile=MLP_TILE, p0=0, final_norm=last)
            pool_p.append(cp)
            pool_s.append(cs)
        else:
            if kind == 1:
                wqkv, wo = mix_w
                keep_p = min(ATTN_WINDOW, L)
                zeros = jnp.zeros((B, ATTN_WINDOW, D), F32)
                xp, kp, vp = _attn(xp, zeros, zeros, g, wqkv, wo, attn_rel_bias[j],
                                   tile=ATTN_TILE[0], nsub=ATTN_NSUB[0], keep=keep_p, mask_start=True)
                ck = cache_attn_k[j].reshape(BS, ATTN_WINDOW, D)
                cv = cache_attn_v[j].reshape(BS, ATTN_WINDOW, D)
                xs, ks, vs = _attn(xs, ck, cv, g, wqkv, wo, attn_rel_bias[j],
                                   tile=ATTN_TILE[1], nsub=ATTN_NSUB[1], keep=LS, mask_start=False)
                k_p.append(kp.reshape(B, keep_p, ATTN_HEADS, ATTN_HD))
                v_p.append(vp.reshape(B, keep_p, ATTN_HEADS, ATTN_HD))
                k_s.append(ks.reshape(BS, LS, ATTN_HEADS, ATTN_HD))
                v_s.append(vs.reshape(BS, LS, ATTN_HEADS, ATTN_HD))
            else:
                w_in, w_o = mix_w
                gn = hgrn_gnorm[j].reshape(1, HG_DV)
                s0 = jnp.zeros((B, HG_HEADS, HG_DK, HG_DV), F32)
                xp, sp = _hgrn(xp, s0, g, w_in, w_o, gn, hgrn_lower_bounds, chunk=HGRN_CHUNK[0],
                               nsub=HGRN_NSUB[0], layer=i)
                xs, ss = _hgrn(xs, state_hgrn[j], g, w_in, w_o, gn, hgrn_lower_bounds, chunk=HGRN_CHUNK[1],
                               nsub=HGRN_NSUB[1], layer=i)
                h_p.append(sp)
                h_s.append(ss)
            xp, xs, cast = _mlp(xp.reshape(1, B * L, D), xs.reshape(1, BS * LS, D), gm, w1, w2, gf,
                                next_casts(i), tile=MLP_TILE, final_norm=last)
            xp = xp.reshape(B, L, D)
        xs = xs.reshape(BS, LS, D)
        if cast:
            w1, w2, mix_w = cast[0], cast[1], cast[2:]

    return (xp, xs, jnp.stack(pool_p), jnp.stack(pool_s), jnp.stack(k_p), jnp.stack(k_s),
            jnp.stack(v_p), jnp.stack(v_s), jnp.stack(h_p), jnp.stack(h_s))
```

```python
import functools
import math

import jax
import jax.numpy as jnp
from jax import lax
from jax.experimental import pallas as pl
from jax.experimental.pallas import tpu as pltpu

F32 = jnp.float32
BF16 = jnp.bfloat16

D_MODEL = 1024
DEPTH = 4
CHUNK = 64
N_MIXERS = 3
PAST_LEN = 2048
EPS = 1e-6
POOL_WINDOWS = (2, 4, 8, 16)
POOL_GROUPS = 4
POOL_GC = D_MODEL // POOL_GROUPS
POOL_HIST = max(POOL_WINDOWS) - 1
POOL_HALO = POOL_HIST + 1
ATTN_HEADS = 16
ATTN_HD = D_MODEL // ATTN_HEADS
N_PAST_CHUNKS = 8
ATTN_WINDOW = N_PAST_CHUNKS * CHUNK
REL_MIN = -(CHUNK - 1)
REL_MAX = 256
REL_SIZE = REL_MAX - REL_MIN + 1
HEAD_PAIRS = ATTN_HEADS // 2
PAIR_W = 2 * ATTN_HD
HG_HEADS = 8
HG_DK = D_MODEL // HG_HEADS
HG_DV = D_MODEL // HG_HEADS
HG_DIAG = 8
HG_SAFE_LOG2 = 100.0
LOG2E = math.log2(math.e)
D_FF = 4 * D_MODEL

V7X_VMEM_BYTES = 64 * 1024 * 1024
VMEM_LIMIT_BYTES = V7X_VMEM_BYTES - 8 * 1024 * 1024

NEG_INF = float("-inf")


def _params():
    return pltpu.CompilerParams(dimension_semantics=("arbitrary", "arbitrary"),
                                vmem_limit_bytes=VMEM_LIMIT_BYTES)


def _whole(shape, single_buffer=False):
    zeros = (0,) * len(shape)
    kw = dict(pipeline_mode=pl.Buffered(1)) if single_buffer else {}
    return pl.BlockSpec(shape, lambda s, t: zeros, **kw)


def _layer(shape, i):
    zeros = (0,) * len(shape)
    return pl.BlockSpec((None,) + tuple(shape), lambda s, t: (i,) + zeros)


def _cast_specs(w, layer, nchunk):
    _, rows, cols = w.shape
    assert rows % nchunk == 0 and (rows // nchunk) % 16 == 0
    cr = rows // nchunk
    src = pl.BlockSpec((None, cr, cols), lambda s, t: (layer, jnp.minimum(t, nchunk - 1), 0))
    dst = pl.BlockSpec((cr, cols), lambda s, t: (jnp.minimum(t, nchunk - 1), 0))
    return src, dst, jax.ShapeDtypeStruct((rows, cols), BF16)


def _cast_chunks(src_refs, dst_refs):
    for s_ref, d_ref in zip(src_refs, dst_refs):
        d_ref[...] = s_ref[...].astype(BF16)


def _rms(x, g):
    ms = jnp.mean(x * x, axis=-1, keepdims=True)
    return x * lax.rsqrt(ms + EPS) * g


def _rms_split(x, g):
    r = lax.rsqrt(jnp.mean(x * x, axis=-1, keepdims=True) + EPS)
    return (x * g).astype(BF16), r


def _dot(a, b):
    return jnp.dot(a, b, preferred_element_type=F32)


def _dot_tn(a, b):
    return lax.dot_general(a, b, (((0,), (0,)), ((), ())), preferred_element_type=F32)


def _split3(x):
    hi = x.astype(BF16)
    r1 = x - hi.astype(F32)
    mid = r1.astype(BF16)
    lo = (r1 - mid.astype(F32)).astype(BF16)
    return hi, mid, lo


def _mlp_apply(x, g, w1_ref, w2_ref, gf, final_norm, nchunk=1):
    xg, r = _rms_split(x, g)
    fc = D_FF // nchunk
    acc = None
    for c in range(nchunk):
        h = _dot(xg, w1_ref[:, c * fc:(c + 1) * fc])
        h = jnp.square(jnp.maximum(h, 0.0)).astype(BF16)
        part = _dot(h, w2_ref[c * fc:(c + 1) * fc, :])
        acc = part if acc is None else acc + part
    y = x + (r * r) * acc
    if final_norm:
        y = _rms(y, gf)
    return y


def _mlp_kernel(xp_ref, xs_ref, g_ref, w1_ref, w2_ref, gf_ref, *rest, final_norm, n_cast):
    op_ref, os_ref = rest[n_cast:n_cast + 2]
    _cast_chunks(rest[:n_cast], rest[n_cast + 2:])
    t = pl.program_id(1)
    last = pl.num_programs(1) - 1

    @pl.when(t < last)
    def _():
        op_ref[0] = _mlp_apply(xp_ref[0], g_ref[...], w1_ref, w2_ref, gf_ref[...], final_norm, nchunk=4)

    @pl.when(t == last)
    def _():
        os_ref[0] = _mlp_apply(xs_ref[0], g_ref[...], w1_ref, w2_ref, gf_ref[...], final_norm)


def _mlp(xp, xs, g, w1, w2, gf, casts, *, tile, final_norm):
    _, Lp, D = xp.shape
    _, Ls, _ = xs.shape
    ntp = Lp // tile
    assert Lp % tile == 0
    tok = pl.BlockSpec((1, tile, D), lambda s, t: (0, jnp.minimum(t, ntp - 1), 0))
    smp = pl.BlockSpec((1, Ls, D), lambda s, t: (0, 0, 0))
    cs = [_cast_specs(w, layer, ntp) for w, layer in casts]
    outs = pl.pallas_call(
        functools.partial(_mlp_kernel, final_norm=final_norm, n_cast=len(casts)),
        grid=(1, ntp + 1),
        in_specs=[tok, smp, _whole((1, D)), _whole((D, D_FF), single_buffer=True),
                  _whole((D_FF, D), single_buffer=True), _whole((1, D))]
        + [c[0] for c in cs],
        out_specs=[tok, smp] + [c[1] for c in cs],
        out_shape=[jax.ShapeDtypeStruct((1, Lp, D), F32), jax.ShapeDtypeStruct((1, Ls, D), F32)]
        + [c[2] for c in cs],
        compiler_params=_params(),
        name="mlp",
    )(xp, xs, g, w1, w2, gf, *[w for w, _ in casts])
    return outs[0], outs[1], list(outs[2:])


def _pool_group(u, pos, ext_ref, w_ref, gi, tile):
    w = POOL_WINDOWS[gi]
    c0, c1 = gi * POOL_GC, (gi + 1) * POOL_GC
    win = u[:, c0:c1]
    for j in range(1, w):
        win = win + ext_ref[POOL_HALO - j:POOL_HALO - j + tile, c0:c1]
    cnt = jnp.minimum(pos + 1, w).astype(F32)
    diff = win / cnt - u[:, c0:c1]
    return _dot(diff.astype(BF16), w_ref[gi])


def _pool_apply(x, pos0, ext_ref, g, w_ref, sc, tile):
    u = _rms(x, g)
    ext_ref[POOL_HALO:POOL_HALO + tile, :] = u
    pos = pos0 + lax.broadcasted_iota(jnp.int32, (tile, 1), 0)
    ys = [_pool_group(u, pos, ext_ref, w_ref, gi, tile) for gi in range(POOL_GROUPS)]
    y = jnp.concatenate(ys, axis=1) * sc
    return x + y, ext_ref[tile:tile + POOL_HALO, :]


def _pool_kernel(x_ref, hist_ref, g_ref, w_ref, sc_ref, o_ref, cache_ref, ext_ref, *, tile, p0):
    t = pl.program_id(1)

    @pl.when(t == 0)
    def _():
        ext_ref[0:POOL_HALO, :] = hist_ref[0]

    out, tail = _pool_apply(x_ref[0], p0 + t * tile, ext_ref, g_ref[...], w_ref, sc_ref[...], tile)
    o_ref[0] = out
    cache_ref[0] = tail
    ext_ref[0:POOL_HALO, :] = tail


def _pool_mlp_kernel(x0_ref, xn_ref, xs_ref, hist_ref, gm_ref, pw_ref, sc_ref, gffn_ref, w1_ref, w2_ref, gf_ref,
                     *rest, tile, p0, final_norm, n_cast):
    o_ref, cache_ref, os_ref = rest[n_cast:n_cast + 3]
    ext_ref, x1_ref = rest[2 * n_cast + 3:]
    _cast_chunks(rest[:n_cast], rest[n_cast + 3:2 * n_cast + 3])
    t = pl.program_id(1)
    nt = pl.num_programs(1) - 1

    @pl.when(t == nt)
    def _():
        os_ref[0] = _mlp_apply(xs_ref[0], gffn_ref[...], w1_ref, w2_ref, gf_ref[...], final_norm)

    @pl.when(t == 0)
    def _():
        ext_ref[0:POOL_HALO, :] = hist_ref[0]
        out, tail = _pool_apply(x0_ref[0], p0, ext_ref, gm_ref[...], pw_ref, sc_ref[...], tile)
        x1_ref[0] = out
        ext_ref[0:POOL_HALO, :] = tail

    @pl.when(t < nt)
    def _():
        x1 = x1_ref[lax.rem(t, 2)]
        v, r = _rms_split(x1, gffn_ref[...])
        halo = ext_ref[0:POOL_HALO, :]
        xn = xn_ref[0]
        u = _rms(xn, gm_ref[...])
        ext_ref[POOL_HALO:POOL_HALO + tile, :] = u
        pos = p0 + (t + 1) * tile + lax.broadcasted_iota(jnp.int32, (tile, 1), 0)
        fc = D_FF // POOL_GROUPS
        acc = None
        ys = []
        for c in range(POOL_GROUPS):
            h = _dot(v, w1_ref[:, c * fc:(c + 1) * fc])
            h = jnp.square(jnp.maximum(h, 0.0)).astype(BF16)
            part = _dot(h, w2_ref[c * fc:(c + 1) * fc, :])
            acc = part if acc is None else acc + part
            ys.append(_pool_group(u, pos, ext_ref, pw_ref, c, tile))
        acc = x1 + (r * r) * acc
        if final_norm:
            acc = _rms(acc, gf_ref[...])
        o_ref[0] = acc

        x1_ref[lax.rem(t + 1, 2)] = xn + jnp.concatenate(ys, axis=1) * sc_ref[...]
        halo = jnp.where(t + 1 < nt, ext_ref[tile:tile + POOL_HALO, :], halo)
        ext_ref[0:POOL_HALO, :] = halo
        cache_ref[0] = halo


def _pool_mlp(x, xs, hist, gm, pw, sc, gffn, w1, w2, gf, casts, *, pool_layer, tile, p0, final_norm):
    S, L, D = x.shape
    _, Ls, _ = xs.shape
    nt = L // tile
    assert S == 1 and L % tile == 0 and tile >= POOL_HALO
    first = pl.BlockSpec((1, tile, D), lambda s, t: (s, 0, 0))
    nxt = pl.BlockSpec((1, tile, D), lambda s, t: (s, jnp.minimum(t + 1, nt - 1), 0))
    tok = pl.BlockSpec((1, tile, D), lambda s, t: (s, jnp.minimum(t, nt - 1), 0))
    smp = pl.BlockSpec((1, Ls, D), lambda s, t: (0, 0, 0))
    seq = pl.BlockSpec((1, POOL_HALO, D), lambda s, t: (s, 0, 0))
    cs = [_cast_specs(w, layer, nt) for w, layer in casts]
    outs = pl.pallas_call(
        functools.partial(_pool_mlp_kernel, tile=tile, p0=p0, final_norm=final_norm, n_cast=len(casts)),
        grid=(S, nt + 1),
        in_specs=[first, nxt, smp, seq, _whole((1, D)), _layer((POOL_GROUPS, POOL_GC, POOL_GC), pool_layer),
                  _whole((1, D)), _whole((1, D)),
                  _whole((D, D_FF), single_buffer=True), _whole((D_FF, D), single_buffer=True), _whole((1, D))]
        + [c[0] for c in cs],
        out_specs=[tok, seq, smp] + [c[1] for c in cs],
        out_shape=[jax.ShapeDtypeStruct((S, L, D), F32), jax.ShapeDtypeStruct((S, POOL_HALO, D), F32),
                   jax.ShapeDtypeStruct((1, Ls, D), F32)] + [c[2] for c in cs],
        scratch_shapes=[pltpu.VMEM((POOL_HALO + tile, D), F32), pltpu.VMEM((2, tile, D), F32)],
        compiler_params=_params(),
        name="pool_mlp",
    )(x, x, xs, hist, gm, pw, sc, gffn, w1, w2, gf, *[w for w, _ in casts])
    return outs[0], outs[1][:, 1:, :], outs[2], list(outs[3:])


def _pool(x, hist, g, w, sc, *, pool_layer, tile, p0):
    S, L, D = x.shape
    assert L % tile == 0 and tile >= POOL_HALO
    tok = pl.BlockSpec((1, tile, D), lambda s, t: (s, t, 0))
    seq = pl.BlockSpec((1, POOL_HALO, D), lambda s, t: (s, 0, 0))
    out, cache = pl.pallas_call(
        functools.partial(_pool_kernel, tile=tile, p0=p0),
        grid=(S, L // tile),
        in_specs=[tok, seq, _whole((1, D)), _layer((POOL_GROUPS, POOL_GC, POOL_GC), pool_layer), _whole((1, D))],
        out_specs=[tok, seq],
        out_shape=[jax.ShapeDtypeStruct((S, L, D), F32), jax.ShapeDtypeStruct((S, POOL_HALO, D), F32)],
        scratch_shapes=[pltpu.VMEM((POOL_HALO + tile, D), F32)],
        compiler_params=_params(),
        name="pool",
    )(x, hist, g, w, sc)
    return out, cache[:, 1:, :]


def _build_attn_bias(rb_ref, bias_ref, *, tile):
    nb = ATTN_WINDOW // tile + 1
    w2 = 2 * tile
    col = lax.broadcasted_iota(jnp.int32, (REL_SIZE, nb * w2), 1)
    row = lax.broadcasted_iota(jnp.int32, (REL_SIZE, nb * w2), 0)
    ip = col % w2
    i = jnp.where(ip < tile, ip, ip - w2)
    idx = jnp.clip((col // w2) * tile - i, REL_MIN, REL_MAX) - REL_MIN
    sel = jnp.where(row == idx, 1.0, 0.0).astype(BF16)
    hi, mid, lo = _split3(rb_ref[...])
    gen = _dot(hi, sel) + _dot(mid, sel) + _dot(lo, sel)

    r = lax.broadcasted_iota(jnp.int32, (tile, w2), 0)
    lane = lax.broadcasted_iota(jnp.int32, (tile, w2), 1)
    first = lane < tile
    for d in range(nb):
        dchunk = (d * tile) // CHUNK + r // CHUNK - (lane % tile) // CHUNK
        ok = (dchunk >= 0) & (dchunk <= N_PAST_CHUNKS)
        for p in range(HEAD_PAIRS):
            ga = jnp.broadcast_to(gen[2 * p:2 * p + 1, d * w2:(d + 1) * w2], (tile, w2))
            gb = jnp.broadcast_to(gen[2 * p + 1:2 * p + 2, d * w2:(d + 1) * w2], (tile, w2))
            ta = pltpu.roll(ga, 0, 1, stride=1, stride_axis=0)
            tb = pltpu.roll(gb, tile, 1, stride=1, stride_axis=0)
            bias_ref[p, d] = jnp.where(ok, jnp.where(first, ta, tb) * LOG2E, NEG_INF)


def _attn_kernel(x_ref, hk_ref, hv_ref, g_ref, wqkv_ref, wo_ref, rb_ref,
                 o_ref, ko_ref, vo_ref, kt_ring, v_ring, bias_ref, *, tile, nsub, mask_start):
    D = D_MODEL
    nb = ATTN_WINDOW // tile + 1
    ring = ATTN_WINDOW // tile + nsub
    t = pl.program_id(1)

    row_lo = lax.broadcasted_iota(jnp.int32, (PAIR_W, tile), 0) < ATTN_HD
    lane_lo = lax.broadcasted_iota(jnp.int32, (tile, PAIR_W), 1) < ATTN_HD
    first = lax.broadcasted_iota(jnp.int32, (tile, 2 * tile), 1) < tile

    @pl.when((pl.program_id(0) == 0) & (t == 0))
    def _():
        _build_attn_bias(rb_ref, bias_ref, tile=tile)

    rr = lax.broadcasted_iota(jnp.int32, (2 * tile, PAIR_W), 0)
    cc = lax.broadcasted_iota(jnp.int32, (2 * tile, PAIR_W), 1)
    sum_cols = jnp.where(((cc == 0) & (rr < tile)) | ((cc == 1) & (rr >= tile)), 1.0, 0.0).astype(BF16)

    def store_block(slot, k, v):
        kt = k.T.astype(BF16)
        vb = v.astype(BF16)
        zk = jnp.zeros((PAIR_W, tile), BF16)
        zv = jnp.zeros((tile, PAIR_W), BF16)
        for p in range(HEAD_PAIRS):
            ktp = kt[p * PAIR_W:(p + 1) * PAIR_W, :]
            kt_ring[slot, p] = jnp.concatenate(
                [jnp.where(row_lo, ktp, zk), jnp.where(row_lo, zk, ktp)], axis=1)
            vp = vb[:, p * PAIR_W:(p + 1) * PAIR_W]
            v_ring[slot, p, :, 0:PAIR_W] = jnp.concatenate(
                [jnp.where(lane_lo, vp, zv), jnp.where(lane_lo, zv, vp)], axis=0)
            v_ring[slot, p, :, PAIR_W:2 * PAIR_W] = sum_cols

    @pl.when(t == 0)
    def _():
        for i in range(nb - 1):
            store_block(i + nsub, hk_ref[0, i * tile:(i + 1) * tile, :], hv_ref[0, i * tile:(i + 1) * tile, :])

    x = x_ref[0]
    xg, r = _rms_split(x, g_ref[...])
    qkv = _dot(xg, wqkv_ref[...])
    q = (qkv[:, :D] * (r * (LOG2E * ATTN_HD ** -0.5))).astype(BF16)
    k = qkv[:, D:2 * D] * r
    v = qkv[:, 2 * D:] * r
    ko_ref[0] = k
    vo_ref[0] = v
    for j in range(nsub):
        store_block(lax.rem(t * nsub + j, ring), k[j * tile:(j + 1) * tile, :], v[j * tile:(j + 1) * tile, :])

    def slot_of(j, d):
        return lax.rem(t * nsub + j + ring - d, ring)

    def score_block(j, p, d):
        s = _dot(q[j * tile:(j + 1) * tile, p * PAIR_W:(p + 1) * PAIR_W], kt_ring[slot_of(j, d), p]) + bias_ref[p, d]
        if mask_start and d > j:
            s = s + jnp.where(t * nsub + j >= d, 0.0, NEG_INF)
        return s

    def row_max(scores):
        mx = functools.reduce(jnp.maximum, scores)
        m_a = jnp.max(mx[:, :tile], axis=-1, keepdims=True)
        m_b = jnp.max(mx[:, tile:], axis=-1, keepdims=True)
        return jnp.where(first, m_a, m_b)

    units = [(j, p) for j in range(nsub) for p in range(HEAD_PAIRS)]
    outs = [[] for _ in range(nsub)]
    nxt = [score_block(0, 0, d) for d in range(nb)]
    for n, (j, p) in enumerate(units):
        scores, m = nxt, row_max(nxt)
        if n + 1 < len(units):
            nxt = [score_block(units[n + 1][0], units[n + 1][1], d) for d in range(nb)]
        acc = jnp.zeros((tile, 2 * PAIR_W), F32)
        for d in range(nb):
            pr = jnp.exp2(scores[d] - m)
            acc = acc + _dot(pr.astype(BF16), v_ring[slot_of(j, d), p])
        l_a = acc[:, PAIR_W:PAIR_W + 1]
        l_b = acc[:, PAIR_W + 1:PAIR_W + 2]
        inv = jnp.where(lane_lo, 1.0 / l_a, 1.0 / l_b)
        outs[j].append((acc[:, :PAIR_W] * inv).astype(BF16))
    o = jnp.concatenate([jnp.concatenate(row, axis=1) for row in outs], axis=0)
    o_ref[0] = x + _dot(o, wo_ref[...])


def _attn(x, hist_k, hist_v, g, wqkv, wo, rel_bias, *, tile, nsub, keep, mask_start):
    S, L, D = x.shape
    step = tile * nsub
    nb = ATTN_WINDOW // tile + 1
    nt = L // step
    nkeep = keep // step
    assert L % step == 0 and tile % CHUNK == 0 and ATTN_WINDOW % tile == 0 and keep % step == 0
    tok = pl.BlockSpec((1, step, D), lambda s, t: (s, t, 0))
    hist = pl.BlockSpec((1, ATTN_WINDOW, D), lambda s, t: (s, 0, 0))
    kv_out = pl.BlockSpec((1, step, D), lambda s, t: (s, jnp.maximum(t - (nt - nkeep), 0), 0))
    return pl.pallas_call(
        functools.partial(_attn_kernel, tile=tile, nsub=nsub, mask_start=mask_start),
        grid=(S, nt),
        in_specs=[tok, hist, hist, _whole((1, D)), _whole((D, 3 * D)), _whole((D, D)),
                  _whole((ATTN_HEADS, REL_SIZE))],
        out_specs=[tok, kv_out, kv_out],
        out_shape=[jax.ShapeDtypeStruct((S, L, D), F32),
                   jax.ShapeDtypeStruct((S, keep, D), F32),
                   jax.ShapeDtypeStruct((S, keep, D), F32)],
        scratch_shapes=[pltpu.VMEM((nb - 1 + nsub, HEAD_PAIRS, PAIR_W, 2 * tile), BF16),
                        pltpu.VMEM((nb - 1 + nsub, HEAD_PAIRS, 2 * tile, 2 * PAIR_W), BF16),
                        pltpu.VMEM((HEAD_PAIRS, nb, tile, 2 * tile), F32)],
        compiler_params=_params(),
        name="attn",
    )(x, hist_k, hist_v, g, wqkv, wo, rel_bias)


def _tile_row(x, n, s):
    C, W = x.shape
    xr = x.reshape(C // n, n, W)
    return jnp.broadcast_to(xr[:, s:s + 1, :], (C // n, n, W)).reshape(C, W)


def _hgrn_intra_fast(q, kk, b2, vb, tril):
    C = q.shape[0]
    bmid = b2[C // 2 - 1:C // 2, :]
    qt = (q * jnp.exp2(b2 - bmid)).astype(BF16)
    kt = (kk * jnp.exp2(bmid - b2)).T.astype(BF16)
    amat = jnp.where(tril, _dot(qt, kt), 0.0)
    return _dot(amat.astype(BF16), vb)


def _hgrn_intra_safe(q, kk, b2, vv, vb, levels, rowid, ones):
    C = q.shape[0]
    amat = jnp.zeros((C, C), F32)
    for hl, upper, same in levels:
        bref = _tile_row(b2, 2 * hl, hl - 1)
        qt = q * jnp.exp2(jnp.where(upper, b2 - bref, NEG_INF))
        kt = kk * jnp.exp2(jnp.where(upper, NEG_INF, bref - b2))
        amat = amat + jnp.where(same, _dot(qt.astype(BF16), kt.T.astype(BF16)), 0.0)
    o = _dot(amat.astype(BF16), vb)
    sub = rowid % HG_DIAG
    for s in range(HG_DIAG):
        dec = jnp.exp2(jnp.where(sub >= s, b2 - _tile_row(b2, HG_DIAG, s), NEG_INF))
        ps = q * _tile_row(kk, HG_DIAG, s) * dec
        o = o + _dot(ps.astype(BF16), ones) * _tile_row(vv, HG_DIAG, s)
    return o


def _hgrn_kernel(x_ref, s0_ref, g_ref, win_ref, wo_ref, gn_ref, lbp_ref,
                 o_ref, so_ref, st_ref, pj_ref, b2_ref, og_ref, *, chunk, nsub, layer, independent):
    D = D_MODEL
    C = chunk
    t = pl.program_id(1)
    nt = pl.num_programs(1)
    nstate = nsub if independent else 1

    @pl.when(t == 0)
    def _():
        for c in range(nstate):
            for h in range(HG_HEADS):
                st_ref[c * HG_HEADS + h] = s0_ref[c, h].T

    x = x_ref[...].reshape(nsub * C, D)
    xg, rinv = _rms_split(x, g_ref[...])
    pj_ref[...] = _dot(xg, win_ref[...]) * rinv

    lbp = lbp_ref[...]
    e = jnp.exp(lbp - jnp.max(lbp, axis=0, keepdims=True))
    sm = e / jnp.sum(e, axis=0, keepdims=True)
    lb = jnp.zeros((1, D), F32)
    for i in range(1, layer + 1):
        lb = lb + sm[i:i + 1, :]
    log_lb = jnp.log(lb)
    log_1mlb = jnp.log1p(-lb)

    zf = pj_ref[:, D:2 * D]
    a = jnp.exp(-jnp.abs(zf))
    r = 1.0 / (1.0 + a)
    log_sig = jnp.minimum(zf, 0.0) - jnp.log(1.0 + a)
    bb = log_1mlb + log_sig
    logf = jnp.maximum(log_lb, bb) + jnp.log(1.0 + jnp.exp(-jnp.abs(log_lb - bb)))
    pj_ref[:, D:2 * D] = (1.0 - lb) * jnp.where(zf >= 0.0, a * r, r)
    zq = pj_ref[:, :D]
    pj_ref[:, :D] = zq * (1.0 / (1.0 + jnp.exp(-zq)))

    ri = lax.broadcasted_iota(jnp.int32, (C, C), 0)
    ci = lax.broadcasted_iota(jnp.int32, (C, C), 1)
    tril = ri >= ci
    tri = tril.astype(BF16)
    ends = []
    safe = None
    for c in range(nsub):
        hi, mid, lo = _split3(logf[c * C:(c + 1) * C, :])
        b2_c = (_dot(tri, hi) + _dot(tri, mid) + _dot(tri, lo)) * LOG2E
        b2_ref[c * C:(c + 1) * C, :] = b2_c
        b2_mid = b2_c[C // 2 - 1:C // 2, :]
        b2_end = b2_c[C - 1:C, :]
        ends.append(b2_end)
        ok = (jnp.min(b2_mid) >= -HG_SAFE_LOG2) & (jnp.min(b2_end - b2_mid) >= -HG_SAFE_LOG2)
        safe = ok if safe is None else safe & ok

    def finish(intra):
        for c in range(nsub):
            rows = slice(c * C, (c + 1) * C)
            for h in range(HG_HEADS):
                cols = slice(h * HG_DK, (h + 1) * HG_DK)
                q, kk, b2 = pj_ref[rows, cols], pj_ref[rows, D + h * HG_DK:D + (h + 1) * HG_DK], b2_ref[rows, cols]
                vv = pj_ref[rows, 2 * D + h * HG_DV:2 * D + (h + 1) * HG_DV]
                zg = pj_ref[rows, 3 * D + h * HG_DV:3 * D + (h + 1) * HG_DV]
                vb = vv.astype(BF16)
                sidx = c * HG_HEADS + h if independent else h
                st = st_ref[sidx]
                end = ends[c][:, cols]
                o = _dot((q * jnp.exp2(b2)).astype(BF16), st.T.astype(BF16))
                kdec = (kk * jnp.exp2(end - b2)).astype(BF16)
                st_ref[sidx] = st * jnp.exp2(end) + _dot_tn(vb, kdec)
                o = o + intra(q, kk, b2, vv, vb)
                on = o * lax.rsqrt(jnp.mean(o * o, axis=-1, keepdims=True) + EPS) * gn_ref[...]
                og_ref[rows, h * HG_DV:(h + 1) * HG_DV] = (on * (zg * (1.0 / (1.0 + jnp.exp(-zg))))).astype(BF16)
        o_ref[...] = (x + _dot(og_ref[...], wo_ref[...])).reshape(o_ref.shape)

    @pl.when(safe)
    def _():
        finish(lambda q, kk, b2, vv, vb: _hgrn_intra_fast(q, kk, b2, vb, tril))

    @pl.when(jnp.logical_not(safe))
    def _():
        rowid = lax.broadcasted_iota(jnp.int32, (C, 1), 0)
        ones = jnp.ones((HG_DK, HG_DV), BF16)
        levels = []
        hl = C // 2
        while hl >= HG_DIAG:
            levels.append((hl, (rowid % (2 * hl)) >= hl, (ri // (2 * hl)) == (ci // (2 * hl))))
            hl //= 2
        finish(lambda q, kk, b2, vv, vb: _hgrn_intra_safe(q, kk, b2, vv, vb, levels, rowid, ones))

    @pl.when(t == nt - 1)
    def _():
        for c in range(nstate):
            for h in range(HG_HEADS):
                so_ref[c, h] = st_ref[c * HG_HEADS + h].T


def _hgrn(x, s0, g, w_in, w_o, gn, lbp, *, chunk, nsub, layer):
    S, L, D = x.shape
    independent = L == chunk
    tile = chunk * nsub
    assert chunk % (2 * HG_DIAG) == 0 and (S % nsub == 0 if independent else L % tile == 0)
    if independent:
        grid = (S // nsub, 1)
        tok = pl.BlockSpec((nsub, chunk, D), lambda s, t: (s, 0, 0))
        state = pl.BlockSpec((nsub, HG_HEADS, HG_DK, HG_DV), lambda s, t: (s, 0, 0, 0))
    else:
        grid = (S, L // tile)
        tok = pl.BlockSpec((1, tile, D), lambda s, t: (s, t, 0))
        state = pl.BlockSpec((1, HG_HEADS, HG_DK, HG_DV), lambda s, t: (s, 0, 0, 0))
    nstate = nsub if independent else 1
    return pl.pallas_call(
        functools.partial(_hgrn_kernel, chunk=chunk, nsub=nsub, layer=layer, independent=independent),
        grid=grid,
        in_specs=[tok, state, _whole((1, D)), _whole((D, 4 * D)), _whole((D, D)), _whole((1, HG_DV)),
                  _whole((DEPTH, D))],
        out_specs=[tok, state],
        out_shape=[jax.ShapeDtypeStruct((S, L, D), F32),
                   jax.ShapeDtypeStruct((S, HG_HEADS, HG_DK, HG_DV), F32)],
        scratch_shapes=[pltpu.VMEM((nstate * HG_HEADS, HG_DV, HG_DK), F32),
                        pltpu.VMEM((tile, 4 * D), F32),
                        pltpu.VMEM((tile, D), F32),
                        pltpu.VMEM((tile, D), BF16)],
        compiler_params=_params(),
        name="hgrn",
    )(x, s0, g, w_in, w_o, gn, lbp)


MLP_TILE = 512
POOL_TILE = (512, 64)
ATTN_TILE = (128, 64)
ATTN_NSUB = (2, 1)
HGRN_CHUNK = (128, 64)
HGRN_NSUB = (4, 4)


def kernel(x_prompt, x_sample, cache_pool, cache_attn_k, cache_attn_v, state_hgrn, norm_mix, norm_ffn,
           norm_final, pool_w, pool_scale, attn_wqkv, attn_wo, attn_rel_bias, hgrn_w_in, hgrn_w_o,
           hgrn_gnorm, hgrn_lower_bounds, mlp_w1, mlp_w2):
    B, L, D = x_prompt.shape
    BS, LS, _ = x_sample.shape
    xp = x_prompt
    xs = x_sample
    gf = norm_final.reshape(1, D)
    def next_casts(i):
        if i + 1 >= DEPTH:
            return []
        kind, j = (i + 1) % N_MIXERS, (i + 1) // N_MIXERS
        casts = [(mlp_w1, i + 1), (mlp_w2, i + 1)]
        if kind == 1:
            casts += [(attn_wqkv, j), (attn_wo, j)]
        elif kind == 2:
            casts += [(hgrn_w_in, j), (hgrn_w_o, j)]
        return casts

    w1 = mlp_w1[0].astype(BF16)
    w2 = mlp_w2[0].astype(BF16)
    mix_w = []
    pw_all = pool_w.astype(BF16)

    pool_p, pool_s, k_p, k_s, v_p, v_s, h_p, h_s = [], [], [], [], [], [], [], []
    for i in range(DEPTH):
        kind, j = i % N_MIXERS, i // N_MIXERS
        g = norm_mix[i].reshape(1, D)
        gm = norm_ffn[i].reshape(1, D)
        last = i == DEPTH - 1
        if kind == 0:
            sc = pool_scale[j].reshape(1, D)
            hist_p = jnp.zeros((B, POOL_HALO, D), F32)
            hist_s = jnp.pad(cache_pool[j], ((0, 0), (1, 0), (0, 0)))
            xs, cs = _pool(xs, hist_s, g, pw_all, sc, pool_layer=j, tile=POOL_TILE[1], p0=PAST_LEN)
            xp, cp, xs, cast = _pool_mlp(xp, xs.reshape(1, BS * LS, D), hist_p, g, pw_all, sc, gm, w1, w2, gf,
                                         next_casts(i), pool_layer=j, tile=MLP_TILE, p0=0, final_norm=last)
            pool_p.append(cp)
            pool_s.append(cs)
        else:
            if kind == 1:
                wqkv, wo = mix_w
                keep_p = min(ATTN_WINDOW, L)
                zeros = jnp.zeros((B, ATTN_WINDOW, D), F32)
                xp, kp, vp = _attn(xp, zeros, zeros, g, wqkv, wo, attn_rel_bias[j],
                                   tile=ATTN_TILE[0], nsub=ATTN_NSUB[0], keep=keep_p, mask_start=True)
                ck = cache_attn_k[j].reshape(BS, ATTN_WINDOW, D)
                cv = cache_attn_v[j].reshape(BS, ATTN_WINDOW, D)
                xs, ks, vs = _attn(xs, ck, cv, g, wqkv, wo, attn_rel_bias[j],
                                   tile=ATTN_TILE[1], nsub=ATTN_NSUB[1], keep=LS, mask_start=False)
                k_p.append(kp.reshape(B, keep_p, ATTN_HEADS, ATTN_HD))
                v_p.append(vp.reshape(B, keep_p, ATTN_HEADS, ATTN_HD))
                k_s.append(ks.reshape(BS, LS, ATTN_HEADS, ATTN_HD))
                v_s.append(vs.reshape(BS, LS, ATTN_HEADS, ATTN_HD))
            else:
                w_in, w_o = mix_w
                gn = hgrn_gnorm[j].reshape(1, HG_DV)
                s0 = jnp.zeros((B, HG_HEADS, HG_DK, HG_DV), F32)
                xp, sp = _hgrn(xp, s0, g, w_in, w_o, gn, hgrn_lower_bounds, chunk=HGRN_CHUNK[0],
                               nsub=HGRN_NSUB[0], layer=i)
                xs, ss = _hgrn(xs, state_hgrn[j], g, w_in, w_o, gn, hgrn_lower_bounds, chunk=HGRN_CHUNK[1],
                               nsub=HGRN_NSUB[1], layer=i)
                h_p.append(sp)
                h_s.append(ss)
            xp, xs, cast = _mlp(xp.reshape(1, B * L, D), xs.reshape(1, BS * LS, D), gm, w1, w2, gf,
                                next_casts(i), tile=2 * MLP_TILE, final_norm=last)
            xp = xp.reshape(B, L, D)
        xs = xs.reshape(BS, LS, D)
        if cast:
            w1, w2, mix_w = cast[0], cast[1], cast[2:]

    return (xp, xs, jnp.stack(pool_p), jnp.stack(pool_s), jnp.stack(k_p), jnp.stack(k_s),
            jnp.stack(v_p), jnp.stack(v_s), jnp.stack(h_p), jnp.stack(h_s))
```
